```python
import math
import jax, jax.numpy as jnp
from jax import lax
import numpy as np

D_MODEL = 2048
BATCH = 16
SEQ = 2048
DEPTH = 4

N_MEM = 256
BLOCK = 128
RMS_EPS = 1e-6

SWA_WINDOW = 128
SWA_HEADS = 16
SWA_KV_HEADS = 2
SWA_HEAD_DIM = 64
SWA_GROUP = SWA_HEADS // SWA_KV_HEADS
SWA_Q = SWA_HEADS * SWA_HEAD_DIM
SWA_KV = SWA_KV_HEADS * SWA_HEAD_DIM

DIFF_HEADS = 8
DIFF_HEAD_DIM = 64
DIFF_V_DIM = 2 * DIFF_HEAD_DIM
DIFF_QK = DIFF_HEADS * 2 * DIFF_HEAD_DIM
DIFF_V = DIFF_HEADS * DIFF_V_DIM
DIFF_SUBLN_EPS = 1e-5

MEM_HEADS = 4
MEM_HEAD_DIM = 256
MEM_Q = MEM_HEADS * MEM_HEAD_DIM

N_BRANCH = 3
BRANCH_WIDTH = 1024

IN_SIZES = (SWA_Q, SWA_KV, SWA_KV, DIFF_QK, DIFF_QK, DIFF_V, MEM_Q, N_BRANCH * D_MODEL)
D_IN = sum(IN_SIZES)
IN_SPLITS = tuple(int(v) for v in np.cumsum(IN_SIZES)[:-1])

D_FF = 4 * D_MODEL

N_BUCKETS = 32
MAX_DISTANCE = 128
N_BIAS_HEADS = SWA_HEADS + DIFF_HEADS

kernel_name = "hybrid_swa_sink_diffattn_memxattn_gated_trunk"


def rms_norm(x, g, eps=RMS_EPS):
    xf = x.astype(jnp.float32)
    y = xf * lax.rsqrt(jnp.mean(xf * xf, axis=-1, keepdims=True) + eps)
    return (y * g.astype(jnp.float32)).astype(x.dtype)


def t5_bucket(dist):
    max_exact = N_BUCKETS // 2
    n = jnp.maximum(dist, 0)
    nf = jnp.maximum(n, 1).astype(jnp.float32)
    large = max_exact + (jnp.log(nf / max_exact) / math.log(MAX_DISTANCE / max_exact)
                         * (N_BUCKETS - max_exact)).astype(jnp.int32)
    large = jnp.minimum(large, N_BUCKETS - 1)
    return jnp.where(n < max_exact, n, large)


def swa_sink_attention(q, k, v, sinks, bias_table):
    B, S = q.shape[0], q.shape[1]
    nb = S // BLOCK
    qb = q.reshape(B, nb, BLOCK, SWA_KV_HEADS, SWA_GROUP, SWA_HEAD_DIM)
    kb = k.reshape(B, nb, BLOCK, SWA_KV_HEADS, SWA_HEAD_DIM)
    vb = v.reshape(B, nb, BLOCK, SWA_KV_HEADS, SWA_HEAD_DIM)

    def with_prev(t):
        prev = jnp.pad(t, ((0, 0), (1, 0), (0, 0), (0, 0), (0, 0)))[:, :-1]
        return jnp.concatenate([prev, t], axis=2)

    kw, vw = with_prev(kb), with_prev(vb)
    s = jnp.einsum('bnqhgd,bnkhd->bnhgqk', qb, kw,
                   preferred_element_type=jnp.float32) * (SWA_HEAD_DIM ** -0.5)

    qi = jnp.arange(BLOCK)[:, None]
    kj = jnp.arange(2 * BLOCK)[None, :]
    dist = qi + BLOCK - kj
    blk = jnp.arange(nb)[:, None, None]
    valid = (dist >= 0) & (dist < SWA_WINDOW) & (blk * BLOCK - BLOCK + kj >= 0)

    bias = bias_table[:, :SWA_HEADS][t5_bucket(dist)].astype(jnp.float32)
    bias = bias.reshape(BLOCK, 2 * BLOCK, SWA_KV_HEADS, SWA_GROUP).transpose(2, 3, 0, 1)
    s = jnp.where(valid[None, :, None, None], s + bias[None, None], -jnp.inf)

    sink = sinks.astype(jnp.float32).reshape(SWA_KV_HEADS, SWA_GROUP)[None, None, :, :, None, None]
    m = jnp.maximum(jnp.max(s, axis=-1, keepdims=True), sink)
    p = jnp.exp(s - m)
    denom = jnp.sum(p, axis=-1, keepdims=True) + jnp.exp(sink - m)
    o = jnp.einsum('bnhgqk,bnkhd->bnqhgd', (p / denom).astype(v.dtype), vw)
    return o.reshape(B, S, SWA_Q)


def diff_attention(q, k, v, lam, lam_init, subln_g, bias_table):
    B, S = q.shape[0], q.shape[1]
    nb = S // BLOCK
    q = q.reshape(B, S, DIFF_HEADS, 2, DIFF_HEAD_DIM)
    k = k.reshape(B, S, DIFF_HEADS, 2, DIFF_HEAD_DIM)
    v = v.reshape(B, S, DIFF_HEADS, DIFF_V_DIM)
    table = bias_table[:, SWA_HEADS:]
    scale = DIFF_HEAD_DIM ** -0.5
    outs = []
    for i in range(nb):
        L = (i + 1) * BLOCK
        q_i = q[:, i * BLOCK:L]
        k_i, v_i = k[:, :L], v[:, :L]
        s = jnp.einsum('bqhmd,bkhmd->bhmqk', q_i, k_i,
                       preferred_element_type=jnp.float32) * scale
        dist = (i * BLOCK + jnp.arange(BLOCK))[:, None] - jnp.arange(L)[None, :]
        bias = table[t5_bucket(dist)].astype(jnp.float32).transpose(2, 0, 1)
        s = jnp.where(dist >= 0, s + bias[None, :, None], -jnp.inf)
        a = jax.nn.softmax(s, axis=-1)
        a = a[:, :, 0] - lam * a[:, :, 1]
        outs.append(jnp.einsum('bhqk,bkhe->bqhe', a.astype(v.dtype), v_i))
    o = jnp.concatenate(outs, axis=1)
    o = rms_norm(o, subln_g, DIFF_SUBLN_EPS) * (1.0 - lam_init)
    return o.reshape(B, S, DIFF_V)


def memory_attention(q, mem_k, mem_v):
    B, S = q.shape[0], q.shape[1]
    qh = q.reshape(B, S, MEM_HEADS, MEM_HEAD_DIM)
    kh = mem_k.reshape(B, -1, MEM_HEADS, MEM_HEAD_DIM)
    vh = mem_v.reshape(B, -1, MEM_HEADS, MEM_HEAD_DIM)
    s = jnp.einsum('bshd,bmhd->bhsm', qh, kh,
                   preferred_element_type=jnp.float32) * (MEM_HEAD_DIM ** -0.5)
    a = jax.nn.softmax(s, axis=-1)
    o = jnp.einsum('bhsm,bmhd->bshd', a.astype(vh.dtype), vh)
    return o.reshape(B, S, MEM_Q)


def setup_inputs(seed: int = 0) -> dict:
    key = jax.random.key(seed)
    ks = jax.random.split(key, 20)
    nrm = lambda k, shape, s: jax.random.normal(k, shape, jnp.float32) * s
    return {
        "x": nrm(ks[0], (BATCH, SEQ, D_MODEL), 1.0),
        "mem": nrm(ks[1], (BATCH, N_MEM, D_MODEL), 1.0),
        "rel_bias": nrm(ks[2], (N_BUCKETS, N_BIAS_HEADS), 0.5),
        "norm1_g": 1.0 + nrm(ks[3], (DEPTH, D_MODEL), 0.02),
        "w_in": nrm(ks[4], (DEPTH, D_MODEL, D_IN), D_MODEL ** -0.5),
        "b_gate": nrm(ks[5], (DEPTH, N_BRANCH * D_MODEL), 0.1),
        "swa_sinks": nrm(ks[6], (DEPTH, SWA_HEADS), 0.5),
        "lam_q1": nrm(ks[7], (DEPTH, DIFF_HEAD_DIM), 0.1),
        "lam_k1": nrm(ks[8], (DEPTH, DIFF_HEAD_DIM), 0.1),
        "lam_q2": nrm(ks[9], (DEPTH, DIFF_HEAD_DIM), 0.1),
        "lam_k2": nrm(ks[10], (DEPTH, DIFF_HEAD_DIM), 0.1),
        "diff_subln_g": 1.0 + nrm(ks[11], (DEPTH, DIFF_V_DIM), 0.02),
        "mem_norm_g": 1.0 + nrm(ks[12], (DEPTH, D_MODEL), 0.02),
        "w_mem_kv": nrm(ks[13], (DEPTH, D_MODEL, 2 * MEM_Q), D_MODEL ** -0.5),
        "w_branch": nrm(ks[14], (DEPTH, N_BRANCH, BRANCH_WIDTH, D_MODEL), BRANCH_WIDTH ** -0.5),
        "w_out": nrm(ks[15], (DEPTH, D_MODEL, D_MODEL), D_MODEL ** -0.5),
        "norm2_g": 1.0 + nrm(ks[16], (DEPTH, D_MODEL), 0.02),
        "w_mlp1": nrm(ks[17], (DEPTH, D_MODEL, D_FF), D_MODEL ** -0.5),
        "w_mlp2": nrm(ks[18], (DEPTH, D_FF, D_MODEL), D_FF ** -0.5),
        "final_g": 1.0 + nrm(ks[19], (D_MODEL,), 0.02),
    }


def reference(x, mem, rel_bias, norm1_g, w_in, b_gate, swa_sinks, lam_q1, lam_k1, lam_q2,
              lam_k2, diff_subln_g, mem_norm_g, w_mem_kv, w_branch, w_out, norm2_g,
              w_mlp1, w_mlp2, final_g):
    B, S = x.shape[0], x.shape[1]
    for l in range(DEPTH):
        h = rms_norm(x, norm1_g[l])
        z = jnp.einsum('bsd,de->bse', h, w_in[l])
        q_a, k_a, v_a, q_b, k_b, v_b, q_c, gate = jnp.split(z, IN_SPLITS, axis=-1)

        o_a = swa_sink_attention(q_a, k_a, v_a, swa_sinks[l], rel_bias)

        lam_init = 0.8 - 0.6 * math.exp(-0.3 * l)
        lam = (jnp.exp(jnp.sum(lam_q1[l].astype(jnp.float32) * lam_k1[l].astype(jnp.float32)))
               - jnp.exp(jnp.sum(lam_q2[l].astype(jnp.float32) * lam_k2[l].astype(jnp.float32)))
               + lam_init)
        o_b = diff_attention(q_b, k_b, v_b, lam, lam_init, diff_subln_g[l], rel_bias)

        mem_kv = jnp.einsum('bmd,de->bme', rms_norm(mem, mem_norm_g[l]), w_mem_kv[l])
        mem_k, mem_v = jnp.split(mem_kv, 2, axis=-1)
        o_c = memory_attention(q_c, mem_k, mem_v)

        gates = jax.nn.sigmoid((gate + b_gate[l]).astype(jnp.float32)).astype(x.dtype)
        gates = gates.reshape(B, S, N_BRANCH, D_MODEL)
        merged = None
        for i, o_i in enumerate((o_a, o_b, o_c)):
            term = gates[:, :, i] * jnp.einsum('bsc,cd->bsd', o_i, w_branch[l, i])
            merged = term if merged is None else merged + term
        x = x + jnp.einsum('bsd,de->bse', merged, w_out[l])

        h2 = rms_norm(x, norm2_g[l])
        u = jnp.square(jax.nn.relu(jnp.einsum('bsd,df->bsf', h2, w_mlp1[l])))
        x = x + jnp.einsum('bsf,fd->bsd', u, w_mlp2[l])
    return rms_norm(x, final_g)
```

```python
import functools
import math

import numpy as np
import jax
import jax.numpy as jnp
from jax import lax
from jax.experimental import pallas as pl
from jax.experimental.pallas import tpu as pltpu

D_MODEL = 2048
DEPTH = 4
N_MEM = 256
RMS_EPS = 1e-6

SWA_WINDOW = 128
SWA_BLOCK = 128
SWA_HEADS = 16
SWA_KV_HEADS = 2
SWA_GROUP = SWA_HEADS // SWA_KV_HEADS
SWA_HEAD_DIM = 64
SWA_Q = SWA_HEADS * SWA_HEAD_DIM
SWA_KV = SWA_KV_HEADS * SWA_HEAD_DIM

DIFF_HEADS = 8
DIFF_HEAD_DIM = 64
DIFF_V_DIM = 2 * DIFF_HEAD_DIM
DIFF_QK = DIFF_HEADS * 2 * DIFF_HEAD_DIM
DIFF_V = DIFF_HEADS * DIFF_V_DIM
DIFF_SUBLN_EPS = 1e-5
DIFF_TILE = 256

MEM_HEADS = 4
MEM_HEAD_DIM = 256
MEM_Q = MEM_HEADS * MEM_HEAD_DIM

N_BRANCH = 3
BRANCH_WIDTH = 1024
GATE_W = N_BRANCH * D_MODEL
D_FF = 4 * D_MODEL

N_BUCKETS = 32
MAX_DISTANCE = 128
N_BIAS_HEADS = SWA_HEADS + DIFF_HEADS

Z_GATE = 0
Z_QA = Z_GATE + GATE_W
Z_KA = Z_QA + SWA_Q
Z_VA = Z_KA + SWA_KV
Z_QB = Z_VA + SWA_KV
Z_KB = Z_QB + DIFF_QK
Z_VB = Z_KB + DIFF_QK
Z_QC = Z_VB + DIFF_V
D_IN = Z_QC + MEM_Q

NEG = -1e30
VMEM_LIMIT_V7X = 56 * 1024 * 1024


def _params(sem, vmem=VMEM_LIMIT_V7X):
    return pltpu.CompilerParams(dimension_semantics=sem, vmem_limit_bytes=vmem)


def _t5_bucket_np(dist):
    max_exact = N_BUCKETS // 2
    n = np.maximum(dist, 0)
    nf = np.maximum(n, 1).astype(np.float64)
    large = max_exact + (np.log(nf / max_exact) / math.log(MAX_DISTANCE / max_exact)
                         * (N_BUCKETS - max_exact)).astype(np.int32)
    large = np.minimum(large, N_BUCKETS - 1)
    return np.where(n < max_exact, n, large).astype(np.int32)


def _swa_bucket_maps():
    qi = np.arange(SWA_BLOCK)[:, None]
    kj = np.arange(2 * SWA_BLOCK)[None, :]
    dist = qi + SWA_BLOCK - kj
    valid = (dist >= 0) & (dist < SWA_WINDOW)
    b = np.where(valid, _t5_bucket_np(dist), -1)
    first = np.where(kj >= SWA_BLOCK, b, -1)
    return np.stack([b, first]).astype(np.int32)


def _diff_bucket_maps():
    t = DIFF_TILE
    qi = np.arange(t)[:, None]
    kj = np.arange(t)[None, :]
    prev = _t5_bucket_np(qi + t - kj)
    d = qi - kj
    diag = np.where(d >= 0, _t5_bucket_np(d), -1)
    far = _t5_bucket_np(np.full((t, t), 2 * t, np.int64))
    return np.stack([prev, diag, far]).astype(np.int32)


def _bias_tiles_kernel(tab_ref, bucket_ref, o_ref, *, head_offset):
    h = pl.program_id(0) + head_offset
    bucket = bucket_ref[...]
    acc = jnp.full(bucket.shape, NEG, jnp.float32)
    for b in range(N_BUCKETS):
        acc = jnp.where(bucket == b, tab_ref[b, h], acc)
    o_ref[...] = acc


def _bias_tiles(rel_bias, bucket_maps, n_heads, head_offset):
    v, r, c = bucket_maps.shape
    return pl.pallas_call(
        functools.partial(_bias_tiles_kernel, head_offset=head_offset),
        grid=(n_heads,),
        in_specs=[pl.BlockSpec(memory_space=pltpu.SMEM),
                  pl.BlockSpec((v, r, c), lambda h: (0, 0, 0))],
        out_specs=pl.BlockSpec((None, v, r, c), lambda h: (h, 0, 0, 0)),
        out_shape=jax.ShapeDtypeStruct((n_heads, v, r, c), jnp.float32),
        compiler_params=_params(("parallel",)),
        name="bias_tiles",
    )(rel_bias, jnp.asarray(bucket_maps))


def _norm_matmul_kernel(x_ref, g_ref, w_ref, b_ref, o_ref, h_ref, *, gate_blocks):
    j = pl.program_id(1)

    @pl.when(j == 0)
    def _():
        xf = x_ref[...]
        ms = jnp.mean(xf * xf, axis=-1, keepdims=True)
        h_ref[...] = (xf * lax.rsqrt(ms + RMS_EPS) * g_ref[...]).astype(h_ref.dtype)

    acc = jnp.dot(h_ref[...], w_ref[...], preferred_element_type=jnp.float32)

    @pl.when(j < gate_blocks)
    def _():
        t = acc + b_ref[...]
        o_ref[...] = (1.0 / (1.0 + jnp.exp(-t))).astype(o_ref.dtype)

    @pl.when(j >= gate_blocks)
    def _():
        o_ref[...] = acc.astype(o_ref.dtype)


def _norm_matmul(x, g, w, layer, b, *, tm, tn, gate_blocks):
    m, k = x.shape
    n = w.shape[-1]
    nb = max(gate_blocks, 1)
    return pl.pallas_call(
        functools.partial(_norm_matmul_kernel, gate_blocks=gate_blocks),
        grid=(m // tm, n // tn),
        in_specs=[pl.BlockSpec((tm, k), lambda i, j: (i, 0)),
                  pl.BlockSpec((None, 1, k), lambda i, j: (layer, 0, 0)),
                  pl.BlockSpec((None, k, tn), lambda i, j: (layer, 0, j)),
                  pl.BlockSpec((None, 1, tn), lambda i, j: (layer, 0, jnp.minimum(j, nb - 1)))],
        out_specs=pl.BlockSpec((tm, tn), lambda i, j: (i, j)),
        out_shape=jax.ShapeDtypeStruct((m, n), jnp.bfloat16),
        scratch_shapes=[pltpu.VMEM((tm, k), jnp.bfloat16)],
        compiler_params=_params(("parallel", "arbitrary")),
        name="norm_matmul",
    )(x, g, w, b)


def _swa_kernel(sink_ref, q_ref, kp_ref, kc_ref, vp_ref, vc_ref, bias_ref, o_ref, *, layer):
    first = (pl.program_id(1) == 0).astype(jnp.int32)
    k = jnp.concatenate([kp_ref[...], kc_ref[...]], axis=0)
    v = jnp.concatenate([vp_ref[...], vc_ref[...]], axis=0)
    q = q_ref[...] * jnp.asarray(SWA_HEAD_DIM ** -0.5, q_ref.dtype)
    outs = []
    for hk in range(SWA_KV_HEADS):
        kh = k[:, hk * SWA_HEAD_DIM:(hk + 1) * SWA_HEAD_DIM]
        vh = v[:, hk * SWA_HEAD_DIM:(hk + 1) * SWA_HEAD_DIM]
        for g in range(SWA_GROUP):
            h = hk * SWA_GROUP + g
            qh = q[:, h * SWA_HEAD_DIM:(h + 1) * SWA_HEAD_DIM]
            s = lax.dot_general(qh, kh, (((1,), (1,)), ((), ())),
                                preferred_element_type=jnp.float32)
            s = s + bias_ref[h, first]
            sink = sink_ref[layer, h]
            m = jnp.maximum(jnp.max(s, axis=-1, keepdims=True), sink)
            p = jnp.exp(s - m)
            denom = jnp.sum(p, axis=-1, keepdims=True) + jnp.exp(sink - m)
            o = jnp.dot(p.astype(vh.dtype), vh, preferred_element_type=jnp.float32)
            outs.append(o / denom)
    o_ref[...] = jnp.concatenate(outs, axis=-1).astype(o_ref.dtype)


def _swa_attention(z3, sinks, bias, layer):
    b, s, _ = z3.shape
    nb = s // SWA_BLOCK
    kcol, vcol = Z_KA // SWA_KV, Z_VA // SWA_KV
    prev = lambda i: jnp.maximum(i - 1, 0)
    return pl.pallas_call(
        functools.partial(_swa_kernel, layer=layer),
        grid=(b, nb),
        in_specs=[pl.BlockSpec(memory_space=pltpu.SMEM),
                  pl.BlockSpec((None, SWA_BLOCK, SWA_Q), lambda bi, i: (bi, i, Z_QA // SWA_Q)),
                  pl.BlockSpec((None, SWA_BLOCK, SWA_KV), lambda bi, i: (bi, prev(i), kcol)),
                  pl.BlockSpec((None, SWA_BLOCK, SWA_KV), lambda bi, i: (bi, i, kcol)),
                  pl.BlockSpec((None, SWA_BLOCK, SWA_KV), lambda bi, i: (bi, prev(i), vcol)),
                  pl.BlockSpec((None, SWA_BLOCK, SWA_KV), lambda bi, i: (bi, i, vcol)),
                  pl.BlockSpec(bias.shape, lambda bi, i: (0, 0, 0, 0))],
        out_specs=pl.BlockSpec((None, SWA_BLOCK, SWA_Q), lambda bi, i: (bi, i, 0)),
        out_shape=jax.ShapeDtypeStruct((b, s, SWA_Q), jnp.bfloat16),
        compiler_params=_params(("parallel", "parallel")),
        name="swa_attention",
    )(sinks, z3, z3, z3, z3, z3, bias)


def _diff_kernel(lam_ref, g_ref, q_ref, k_ref, v_ref, bias_ref, o_ref, s_scr, *, layer):
    t = DIFF_TILE
    i = pl.program_id(2)
    q = q_ref[...]
    lane = lax.broadcasted_iota(jnp.int32, q.shape, 1)
    zero = jnp.zeros_like(q)
    qs = jnp.concatenate([jnp.where(lane < DIFF_HEAD_DIM, q, zero),
                          jnp.where(lane >= DIFF_HEAD_DIM, q, zero)], axis=0)
    qs = qs * jnp.asarray(DIFF_HEAD_DIM ** -0.5, qs.dtype)

    def tile_of(c):
        return jnp.where(c == i, 1, jnp.where(c == i - 1, 0, 2))

    def scores(c, mrun):
        kc = k_ref[pl.ds(pl.multiple_of(c * t, t), t), :]
        s = lax.dot_general(qs, kc, (((1,), (1,)), ((), ())), preferred_element_type=jnp.float32)
        s = (s.reshape(2, t, t) + bias_ref[tile_of(c)][None]).reshape(2 * t, t)
        s_scr[c] = s
        return jnp.maximum(mrun, jnp.maximum(s[:, :128], s[:, 128:]))

    mrun = lax.fori_loop(0, i + 1, scores, jnp.full((2 * t, 128), NEG, jnp.float32))
    m = jnp.max(mrun, axis=-1, keepdims=True)

    def weighted(c, carry):
        lrun, acc = carry
        p = jnp.exp(s_scr[c] - m)
        vc = v_ref[pl.ds(pl.multiple_of(c * t, t), t), :]
        acc = acc + jnp.dot(p.astype(vc.dtype), vc, preferred_element_type=jnp.float32)
        return lrun + (p[:, :128] + p[:, 128:]), acc

    lrun, acc = lax.fori_loop(0, i + 1, weighted,
                              (jnp.zeros((2 * t, 128), jnp.float32),
                               jnp.zeros((2 * t, DIFF_V_DIM), jnp.float32)))
    o2 = acc / jnp.sum(lrun, axis=-1, keepdims=True)

    lam_init = 0.8 - 0.6 * math.exp(-0.3 * layer)
    lv = lam_ref[...]
    lam = (jnp.exp(jnp.sum(lv[0:1] * lv[1:2], axis=-1, keepdims=True))
           - jnp.exp(jnp.sum(lv[2:3] * lv[3:4], axis=-1, keepdims=True)) + lam_init)
    o = o2[:t] - lam * o2[t:]
    ms = jnp.mean(o * o, axis=-1, keepdims=True)
    o = o * lax.rsqrt(ms + DIFF_SUBLN_EPS) * g_ref[...]
    o_ref[...] = (o * (1.0 - lam_init)).astype(o_ref.dtype)


def _diff_attention(z3, lam_vecs, subln_g, bias, layer):
    b, s, _ = z3.shape
    t = DIFF_TILE
    w = DIFF_V_DIM
    return pl.pallas_call(
        functools.partial(_diff_kernel, layer=layer),
        grid=(b, DIFF_HEADS, s // t),
        in_specs=[pl.BlockSpec((None, 4, DIFF_HEAD_DIM), lambda bi, h, i: (layer, 0, 0)),
                  pl.BlockSpec((None, 1, w), lambda bi, h, i: (layer, 0, 0)),
                  pl.BlockSpec((None, t, w), lambda bi, h, i: (bi, i, Z_QB // w + h)),
                  pl.BlockSpec((None, s, w), lambda bi, h, i: (bi, 0, Z_KB // w + h)),
                  pl.BlockSpec((None, s, w), lambda bi, h, i: (bi, 0, Z_VB // w + h)),
                  pl.BlockSpec((None, 3, t, t), lambda bi, h, i: (h, 0, 0, 0))],
        out_specs=pl.BlockSpec((None, t, w), lambda bi, h, i: (bi, i, h)),
        out_shape=jax.ShapeDtypeStruct((b, s, DIFF_V), jnp.bfloat16),
        scratch_shapes=[pltpu.VMEM((s // t, 2 * t, t), jnp.float32)],
        compiler_params=_params(("parallel", "parallel", "arbitrary")),
        name="diff_attention",
    )(lam_vecs, subln_g, z3, z3, z3, bias)


def _mem_kernel(q_ref, k_ref, v_ref, o_ref):
    q = q_ref[...] * jnp.asarray(MEM_HEAD_DIM ** -0.5, q_ref.dtype)
    s = lax.dot_general(q, k_ref[...], (((1,), (1,)), ((), ())), preferred_element_type=jnp.float32)
    m = jnp.max(s, axis=-1, keepdims=True)
    p = jnp.exp(s - m)
    l = jnp.sum(p, axis=-1, keepdims=True)
    o = jnp.dot(p.astype(v_ref.dtype), v_ref[...], preferred_element_type=jnp.float32)
    o_ref[...] = (o / l).astype(o_ref.dtype)


def _mem_attention(z3, mem_kv3, *, tq):
    b, s, _ = z3.shape
    w = MEM_HEAD_DIM
    return pl.pallas_call(
        _mem_kernel,
        grid=(b, MEM_HEADS, s // tq),
        in_specs=[pl.BlockSpec((None, tq, w), lambda bi, h, i: (bi, i, Z_QC // w + h)),
                  pl.BlockSpec((None, N_MEM, w), lambda bi, h, i: (bi, 0, h)),
                  pl.BlockSpec((None, N_MEM, w), lambda bi, h, i: (bi, 0, MEM_HEADS + h))],
        out_specs=pl.BlockSpec((None, tq, w), lambda bi, h, i: (bi, i, h)),
        out_shape=jax.ShapeDtypeStruct((b, s, MEM_Q), jnp.bfloat16),
        compiler_params=_params(("parallel", "parallel", "parallel")),
        name="mem_attention",
    )(z3, mem_kv3, mem_kv3)


def _merge_kernel(x_ref, gate_ref, oa_ref, ob_ref, oc_ref, wb_ref, wo_ref, o_ref):
    merged = None
    for br, br_ref in enumerate((oa_ref, ob_ref, oc_ref)):
        proj = jnp.dot(br_ref[...], wb_ref[br], preferred_element_type=jnp.float32)
        gate = gate_ref[:, br * D_MODEL:(br + 1) * D_MODEL].astype(jnp.float32)
        term = gate * proj
        merged = term if merged is None else merged + term
    y = jnp.dot(merged.astype(wo_ref.dtype), wo_ref[...], preferred_element_type=jnp.float32)
    o_ref[...] = x_ref[...] + y


def _merge_out(x, z, o_a, o_b, o_c, w_branch, w_out, layer, *, tm):
    m, d = x.shape
    once = pl.Buffered(1)
    return pl.pallas_call(
        _merge_kernel,
        grid=(m // tm,),
        in_specs=[pl.BlockSpec((tm, d), lambda i: (i, 0)),
                  pl.BlockSpec((tm, GATE_W), lambda i: (i, Z_GATE // GATE_W)),
                  pl.BlockSpec((tm, BRANCH_WIDTH), lambda i: (i, 0)),
                  pl.BlockSpec((tm, BRANCH_WIDTH), lambda i: (i, 0)),
                  pl.BlockSpec((tm, BRANCH_WIDTH), lambda i: (i, 0)),
                  pl.BlockSpec((None, N_BRANCH, BRANCH_WIDTH, d), lambda i: (layer, 0, 0, 0),
                               pipeline_mode=once),
                  pl.BlockSpec((None, d, d), lambda i: (layer, 0, 0), pipeline_mode=once)],
        out_specs=pl.BlockSpec((tm, d), lambda i: (i, 0)),
        out_shape=jax.ShapeDtypeStruct((m, d), jnp.float32),
        compiler_params=_params(("parallel",)),
        name="merge_out",
    )(x, z, o_a, o_b, o_c, w_branch, w_out)


def _mlp_kernel(x_ref, g_ref, w1_ref, w2_ref, fg_ref, o_ref, h_ref, *, final_norm):
    j = pl.program_id(1)

    @pl.when(j == 0)
    def _():
        xf = x_ref[...]
        ms = jnp.mean(xf * xf, axis=-1, keepdims=True)
        h_ref[...] = (xf * lax.rsqrt(ms + RMS_EPS) * g_ref[...]).astype(h_ref.dtype)
        o_ref[...] = xf

    u = jnp.dot(h_ref[...], w1_ref[...], preferred_element_type=jnp.float32)
    u = jnp.square(jnp.maximum(u, 0.0)).astype(w2_ref.dtype)
    o_ref[...] += jnp.dot(u, w2_ref[...], preferred_element_type=jnp.float32)

    if final_norm:
        @pl.when(j == pl.num_programs(1) - 1)
        def _():
            y = o_ref[...]
            ms = jnp.mean(y * y, axis=-1, keepdims=True)
            o_ref[...] = y * lax.rsqrt(ms + RMS_EPS) * fg_ref[...]


def _mlp(x, g, w1, w2, final_g, layer, *, tm, tf, final_norm):
    m, d = x.shape
    f = w1.shape[-1]
    return pl.pallas_call(
        functools.partial(_mlp_kernel, final_norm=final_norm),
        grid=(m // tm, f // tf),
        in_specs=[pl.BlockSpec((tm, d), lambda i, j: (i, 0)),
                  pl.BlockSpec((None, 1, d), lambda i, j: (layer, 0, 0)),
                  pl.BlockSpec((None, d, tf), lambda i, j: (layer, 0, j)),
                  pl.BlockSpec((None, tf, d), lambda i, j: (layer, j, 0)),
                  pl.BlockSpec((1, d), lambda i, j: (0, 0))],
        out_specs=pl.BlockSpec((tm, d), lambda i, j: (i, 0)),
        out_shape=jax.ShapeDtypeStruct((m, d), jnp.float32),
        scratch_shapes=[pltpu.VMEM((tm, d), jnp.bfloat16)],
        compiler_params=_params(("parallel", "arbitrary")),
        name="mlp",
    )(x, g, w1, w2, final_g)


def kernel(x, mem, rel_bias, norm1_g, w_in, b_gate, swa_sinks, lam_q1, lam_k1, lam_q2, lam_k2,
           diff_subln_g, mem_norm_g, w_mem_kv, w_branch, w_out, norm2_g, w_mlp1, w_mlp2, final_g):
    bsz, seq, d = x.shape
    m = bsz * seq
    bf = jnp.bfloat16

    gate_col = w_in.shape[-1] - GATE_W
    w_in_z = jnp.concatenate([w_in[..., gate_col:], w_in[..., :gate_col]], axis=-1).astype(bf)
    w_mem_kv_b = w_mem_kv.astype(bf)
    w_branch_b = w_branch.astype(bf)
    w_out_b = w_out.astype(bf)
    w_mlp1_b = w_mlp1.astype(bf)
    w_mlp2_b = w_mlp2.astype(bf)

    norm1_g3 = norm1_g.reshape(DEPTH, 1, d)
    norm2_g3 = norm2_g.reshape(DEPTH, 1, d)
    mem_norm_g3 = mem_norm_g.reshape(DEPTH, 1, d)
    b_gate3 = b_gate.reshape(DEPTH, 1, GATE_W)
    subln_g3 = diff_subln_g.reshape(DEPTH, 1, DIFF_V_DIM)
    lam_vecs = jnp.stack([lam_q1, lam_k1, lam_q2, lam_k2], axis=1)
    final_g2 = final_g.reshape(1, d)
    no_bias = jnp.zeros((DEPTH, 1, 1024), jnp.float32)

    swa_bias = _bias_tiles(rel_bias, _swa_bucket_maps(), SWA_HEADS, 0)
    diff_bias = _bias_tiles(rel_bias, _diff_bucket_maps(), DIFF_HEADS, SWA_HEADS)

    x2 = x.reshape(m, d)
    mem2 = mem.reshape(bsz * N_MEM, d)
    tn_in = 768
    for layer in range(DEPTH):
        z = _norm_matmul(x2, norm1_g3, w_in_z, layer, b_gate3, tm=1024, tn=tn_in,
                         gate_blocks=GATE_W // tn_in)
        z3 = z.reshape(bsz, seq, D_IN)
        mem_kv = _norm_matmul(mem2, mem_norm_g3, w_mem_kv_b, layer, no_bias, tm=1024, tn=1024,
                              gate_blocks=0)
        o_a = _swa_attention(z3, swa_sinks, swa_bias, layer)
        o_b = _diff_attention(z3, lam_vecs, subln_g3, diff_bias, layer)
        o_c = _mem_attention(z3, mem_kv.reshape(bsz, N_MEM, 2 * MEM_Q), tq=1024)
        x2 = _merge_out(x2, z, o_a.reshape(m, SWA_Q), o_b.reshape(m, DIFF_V), o_c.reshape(m, MEM_Q),
                        w_branch_b, w_out_b, layer, tm=256)
        x2 = _mlp(x2, norm2_g3, w_mlp1_b, w_mlp2_b, final_g2, layer, tm=1024, tf=512,
                  final_norm=(layer == DEPTH - 1))
    return x2.reshape(bsz, seq, d)
```

```python
import functools
import math

import numpy as np
import jax
import jax.numpy as jnp
from jax import lax
from jax.experimental import pallas as pl
from jax.experimental.pallas import tpu as pltpu

D_MODEL = 2048
DEPTH = 4
N_MEM = 256
RMS_EPS = 1e-6

SWA_WINDOW = 128
SWA_BLOCK = 128
SWA_HEADS = 16
SWA_KV_HEADS = 2
SWA_GROUP = SWA_HEADS // SWA_KV_HEADS
SWA_HEAD_DIM = 64
SWA_Q = SWA_HEADS * SWA_HEAD_DIM
SWA_KV = SWA_KV_HEADS * SWA_HEAD_DIM

DIFF_HEADS = 8
DIFF_HEAD_DIM = 64
DIFF_V_DIM = 2 * DIFF_HEAD_DIM
DIFF_QK = DIFF_HEADS * 2 * DIFF_HEAD_DIM
DIFF_V = DIFF_HEADS * DIFF_V_DIM
DIFF_SUBLN_EPS = 1e-5
DIFF_TILE = 256

MEM_HEADS = 4
MEM_HEAD_DIM = 256
MEM_Q = MEM_HEADS * MEM_HEAD_DIM

N_BRANCH = 3
BRANCH_WIDTH = 1024
GATE_W = N_BRANCH * D_MODEL
D_FF = 4 * D_MODEL

N_BUCKETS = 32
MAX_DISTANCE = 128
N_BIAS_HEADS = SWA_HEADS + DIFF_HEADS

Z_GATE = 0
Z_QA = Z_GATE + GATE_W
Z_KA = Z_QA + SWA_Q
Z_VA = Z_KA + SWA_KV
Z_QB = Z_VA + SWA_KV
Z_KB = Z_QB + DIFF_QK
Z_VB = Z_KB + DIFF_QK
Z_QC = Z_VB + DIFF_V
D_IN = Z_QC + MEM_Q

NEG = -1e30
LOG2E = math.log2(math.e)
BF16_SUBLANES = 16
VMEM_LIMIT_V7X = 56 * 1024 * 1024


def _params(sem, vmem=VMEM_LIMIT_V7X):
    return pltpu.CompilerParams(dimension_semantics=sem, vmem_limit_bytes=vmem)


def _z_column_scale():
    cs = np.ones((1, D_IN), np.float32)
    cs[0, Z_QA:Z_QA + SWA_Q] = SWA_HEAD_DIM ** -0.5 * LOG2E
    cs[0, Z_QB:Z_QB + DIFF_QK] = DIFF_HEAD_DIM ** -0.5 * LOG2E
    cs[0, Z_QC:Z_QC + MEM_Q] = MEM_HEAD_DIM ** -0.5 * LOG2E
    return cs


def _column_max(s):
    r, c = s.shape
    return jnp.max(jnp.max(s.reshape(r // 8, 8, c), axis=0), axis=0, keepdims=True)


def _t5_bucket_np(dist):
    max_exact = N_BUCKETS // 2
    n = np.maximum(dist, 0)
    nf = np.maximum(n, 1).astype(np.float64)
    large = max_exact + (np.log(nf / max_exact) / math.log(MAX_DISTANCE / max_exact)
                         * (N_BUCKETS - max_exact)).astype(np.int32)
    large = np.minimum(large, N_BUCKETS - 1)
    return np.where(n < max_exact, n, large).astype(np.int32)


def _swa_bucket_maps():
    kj = np.arange(2 * SWA_BLOCK)[:, None]
    qi = np.arange(SWA_BLOCK)[None, :]
    dist = qi + SWA_BLOCK - kj
    valid = (dist >= 0) & (dist < SWA_WINDOW)
    b = np.where(valid, _t5_bucket_np(dist), -1)
    first = np.where(kj >= SWA_BLOCK, b, -1)
    return np.stack([b, first]).astype(np.int32)


def _diff_bucket_maps():
    t = DIFF_TILE
    kj = np.arange(t)[:, None]
    qi = np.arange(t)[None, :]
    prev = _t5_bucket_np(qi + t - kj)
    d = qi - kj
    diag = np.where(d >= 0, _t5_bucket_np(d), -1)
    return np.stack([prev, diag]).astype(np.int32)


def _bias_tiles_kernel(tab_ref, bucket_ref, o_ref, *, head_offset, relative):
    h = pl.program_id(0) + head_offset
    bucket = bucket_ref[...]
    base = tab_ref[N_BUCKETS - 1, h] if relative else 0.0
    acc = jnp.full(bucket.shape, NEG, jnp.float32)
    for b in range(N_BUCKETS):
        acc = jnp.where(bucket == b, (tab_ref[b, h] - base) * LOG2E, acc)
    o_ref[...] = acc


def _bias_tiles(rel_bias, bucket_maps, n_heads, head_offset, relative):
    v, r, c = bucket_maps.shape
    return pl.pallas_call(
        functools.partial(_bias_tiles_kernel, head_offset=head_offset, relative=relative),
        grid=(n_heads,),
        in_specs=[pl.BlockSpec(memory_space=pltpu.SMEM),
                  pl.BlockSpec((v, r, c), lambda h: (0, 0, 0))],
        out_specs=pl.BlockSpec((None, v, r, c), lambda h: (h, 0, 0, 0)),
        out_shape=jax.ShapeDtypeStruct((n_heads, v, r, c), jnp.float32),
        compiler_params=_params(("parallel",)),
        name="bias_tiles",
    )(rel_bias, jnp.asarray(bucket_maps))


def _store_rmsnorm(h_ref, x_ref, g_ref):
    xf = x_ref[...]
    ms = jnp.mean(xf * xf, axis=-1, keepdims=True)
    h_ref[...] = (xf * lax.rsqrt(ms + RMS_EPS) * g_ref[...]).astype(h_ref.dtype)


def _proj_in_kernel(x_ref, g_ref, w_ref, b_ref, cs_ref, o_ref, h_ref, *, gate_blocks):
    j = pl.program_id(1)

    @pl.when(j == 0)
    def _():
        _store_rmsnorm(h_ref, x_ref, g_ref)

    acc = jnp.dot(h_ref[...], w_ref[...], preferred_element_type=jnp.float32)

    gate = 1.0 / (1.0 + jnp.exp(-(acc + b_ref[...])))
    o_ref[...] = jnp.where(j < gate_blocks, gate, acc * cs_ref[...]).astype(o_ref.dtype)


def _proj_in(x, g, w, b, col_scale, layer, *, tm, tn):
    m, k = x.shape
    n = w.shape[-1]
    gate_blocks = GATE_W // tn
    return pl.pallas_call(
        functools.partial(_proj_in_kernel, gate_blocks=gate_blocks),
        grid=(m // tm, n // tn),
        in_specs=[pl.BlockSpec((tm, k), lambda i, j: (i, 0)),
                  pl.BlockSpec((None, 1, k), lambda i, j: (layer, 0, 0)),
                  pl.BlockSpec((None, k, tn), lambda i, j: (layer, 0, j)),
                  pl.BlockSpec((None, 1, tn), lambda i, j: (layer, 0, jnp.minimum(j, gate_blocks - 1))),
                  pl.BlockSpec((1, tn), lambda i, j: (0, j))],
        out_specs=pl.BlockSpec((tm, tn), lambda i, j: (i, j)),
        out_shape=jax.ShapeDtypeStruct((m, n), jnp.bfloat16),
        scratch_shapes=[pltpu.VMEM((tm, k), jnp.bfloat16)],
        compiler_params=_params(("parallel", "arbitrary")),
        name="proj_in",
    )(x, g, w, b, col_scale)


def _norm_matmul_kernel(x_ref, g_ref, w_ref, o_ref, h_ref):
    @pl.when(pl.program_id(1) == 0)
    def _():
        _store_rmsnorm(h_ref, x_ref, g_ref)

    o_ref[...] = jnp.dot(h_ref[...], w_ref[...], preferred_element_type=jnp.float32).astype(o_ref.dtype)


def _norm_matmul(x, g, w, layer, *, tm, tn):
    m, k = x.shape
    n = w.shape[-1]
    return pl.pallas_call(
        _norm_matmul_kernel,
        grid=(m // tm, n // tn),
        in_specs=[pl.BlockSpec((tm, k), lambda i, j: (i, 0)),
                  pl.BlockSpec((None, 1, k), lambda i, j: (layer, 0, 0)),
                  pl.BlockSpec((None, k, tn), lambda i, j: (layer, 0, j))],
        out_specs=pl.BlockSpec((tm, tn), lambda i, j: (i, j)),
        out_shape=jax.ShapeDtypeStruct((m, n), jnp.bfloat16),
        scratch_shapes=[pltpu.VMEM((tm, k), jnp.bfloat16)],
        compiler_params=_params(("parallel", "arbitrary")),
        name="norm_matmul",
    )(x, g, w)


def _swa_kernel(sink_ref, q_ref, kp_ref, kc_ref, vp_ref, vc_ref, bias_ref, o_ref, *, layer):
    first = (pl.program_id(1) == 0).astype(jnp.int32)
    hd = SWA_HEAD_DIM
    k = jnp.concatenate([kp_ref[...], kc_ref[...]], axis=0)
    v = jnp.concatenate([vp_ref[...], vc_ref[...]], axis=0)
    v_t = v.astype(jnp.float32).T.astype(v.dtype)
    ones = jnp.ones((BF16_SUBLANES, 2 * SWA_BLOCK), v.dtype)
    q = q_ref[...]
    outs = []
    for hk in range(SWA_KV_HEADS):
        heads = range(hk * SWA_GROUP, (hk + 1) * SWA_GROUP)
        qs = jnp.concatenate([q[:, h * hd:(h + 1) * hd] for h in heads], axis=0)
        s = lax.dot_general(k[:, hk * hd:(hk + 1) * hd], qs, (((1,), (1,)), ((), ())),
                            preferred_element_type=jnp.float32)
        s = s + jnp.concatenate([bias_ref[h, first] for h in heads], axis=1)
        sink = jnp.concatenate([jnp.full((1, SWA_BLOCK), sink_ref[layer, h] * LOG2E, jnp.float32)
                                for h in heads], axis=1)
        m = jnp.maximum(_column_max(s), sink)
        p = jnp.exp2(s - m).astype(v.dtype)
        v_ext = jnp.concatenate([v_t[hk * hd:(hk + 1) * hd], ones], axis=0)
        o_ext = jnp.dot(v_ext, p, preferred_element_type=jnp.float32)
        denom = o_ext[hd:hd + 1] + jnp.exp2(sink - m)
        o_t = o_ext[:hd] / denom
        outs += [o_t[:, g * SWA_BLOCK:(g + 1) * SWA_BLOCK].T for g in range(SWA_GROUP)]
    o_ref[...] = jnp.concatenate(outs, axis=1).astype(o_ref.dtype)


def _swa_attention(z3, sinks, bias, layer):
    b, s, _ = z3.shape
    nb = s // SWA_BLOCK
    kcol, vcol = Z_KA // SWA_KV, Z_VA // SWA_KV
    prev = lambda i: jnp.maximum(i - 1, 0)
    return pl.pallas_call(
        functools.partial(_swa_kernel, layer=layer),
        grid=(b, nb),
        in_specs=[pl.BlockSpec(memory_space=pltpu.SMEM),
                  pl.BlockSpec((None, SWA_BLOCK, SWA_Q), lambda bi, i: (bi, i, Z_QA // SWA_Q)),
                  pl.BlockSpec((None, SWA_BLOCK, SWA_KV), lambda bi, i: (bi, prev(i), kcol)),
                  pl.BlockSpec((None, SWA_BLOCK, SWA_KV), lambda bi, i: (bi, i, kcol)),
                  pl.BlockSpec((None, SWA_BLOCK, SWA_KV), lambda bi, i: (bi, prev(i), vcol)),
                  pl.BlockSpec((None, SWA_BLOCK, SWA_KV), lambda bi, i: (bi, i, vcol)),
                  pl.BlockSpec(bias.shape, lambda bi, i: (0, 0, 0, 0), pipeline_mode=pl.Buffered(1))],
        out_specs=pl.BlockSpec((None, SWA_BLOCK, SWA_Q), lambda bi, i: (bi, i, 0)),
        out_shape=jax.ShapeDtypeStruct((b, s, SWA_Q), jnp.bfloat16),
        compiler_params=_params(("parallel", "parallel")),
        name="swa_attention",
    )(sinks, z3, z3, z3, z3, z3, bias)


def _diff_kernel(lam_ref, g_ref, q_ref, k_ref, v_ref, bias_ref, o_ref, *, layer):
    t = DIFF_TILE
    seq = q_ref.shape[0]
    lam_init = 0.8 - 0.6 * math.exp(-0.3 * layer)
    lv = lam_ref[...]
    lam = (jnp.exp(jnp.sum(lv[0:1] * lv[1:2], axis=-1, keepdims=True))
           - jnp.exp(jnp.sum(lv[2:3] * lv[3:4], axis=-1, keepdims=True)) + lam_init)

    v_t = v_ref[...].astype(jnp.float32).T.astype(v_ref.dtype)
    ones = jnp.ones((BF16_SUBLANES, seq), v_ref.dtype)
    v_ext = jnp.concatenate([v_t, ones], axis=0)
    prev2 = jnp.concatenate([bias_ref[0], bias_ref[0]], axis=1)
    diag2 = jnp.concatenate([bias_ref[1], bias_ref[1]], axis=1)
    lane = lax.broadcasted_iota(jnp.int32, (t, 2 * DIFF_HEAD_DIM), 1)
    n_blocks = seq // t

    @functools.cache
    def stacked_queries(i):
        q = q_ref[i * t:(i + 1) * t, :]
        zero = jnp.zeros_like(q)
        return jnp.concatenate([jnp.where(lane < DIFF_HEAD_DIM, q, zero),
                                jnp.where(lane >= DIFF_HEAD_DIM, q, zero)], axis=0)

    def score_chunk(i, c):
        s = lax.dot_general(k_ref[c * t:(c + 1) * t, :], stacked_queries(i), (((1,), (1,)), ((), ())),
                            preferred_element_type=jnp.float32)
        return s + diag2 if c == i else s + prev2 if c == i - 1 else s

    chunks = [score_chunk(0, 0)]
    for i in range(n_blocks):
        m = functools.reduce(jnp.maximum, [_column_max(s) for s in chunks])
        acc = None
        next_chunks = []
        for c in range(i + 2):
            if i + 1 < n_blocks:
                next_chunks.append(score_chunk(i + 1, c))
            if c <= i:
                p = jnp.exp2(chunks[c] - m).astype(v_ref.dtype)
                pv = jnp.dot(v_ext[:, c * t:(c + 1) * t], p, preferred_element_type=jnp.float32)
                acc = pv if acc is None else acc + pv
        chunks = next_chunks
        o2_t = acc[:DIFF_V_DIM] / acc[DIFF_V_DIM:DIFF_V_DIM + 1]
        o = (o2_t[:, :t] - lam * o2_t[:, t:]).T
        ms = jnp.mean(o * o, axis=-1, keepdims=True)
        o = o * lax.rsqrt(ms + DIFF_SUBLN_EPS) * g_ref[...]
        o_ref[i * t:(i + 1) * t, :] = (o * (1.0 - lam_init)).astype(o_ref.dtype)


def _diff_attention(z3, lam_vecs, subln_g, bias, layer):
    b, s, _ = z3.shape
    t = DIFF_TILE
    w = DIFF_V_DIM
    return pl.pallas_call(
        functools.partial(_diff_kernel, layer=layer),
        grid=(b, DIFF_HEADS),
        in_specs=[pl.BlockSpec((None, 4, DIFF_HEAD_DIM), lambda bi, h: (layer, 0, 0)),
                  pl.BlockSpec((None, 1, w), lambda bi, h: (layer, 0, 0)),
                  pl.BlockSpec((None, s, w), lambda bi, h: (bi, 0, Z_QB // w + h)),
                  pl.BlockSpec((None, s, w), lambda bi, h: (bi, 0, Z_KB // w + h)),
                  pl.BlockSpec((None, s, w), lambda bi, h: (bi, 0, Z_VB // w + h)),
                  pl.BlockSpec((None, 2, t, t), lambda bi, h: (h, 0, 0, 0))],
        out_specs=pl.BlockSpec((None, s, w), lambda bi, h: (bi, 0, h)),
        out_shape=jax.ShapeDtypeStruct((b, s, DIFF_V), jnp.bfloat16),
        compiler_params=_params(("parallel", "parallel")),
        name="diff_attention",
    )(lam_vecs, subln_g, z3, z3, z3, bias)


def _mem_kernel(q_ref, k_ref, v_ref, o_ref):
    s = lax.dot_general(q_ref[...], k_ref[...], (((1,), (1,)), ((), ())),
                        preferred_element_type=jnp.float32)
    m = jnp.max(s, axis=-1, keepdims=True)
    p = jnp.exp2(s - m)
    l = jnp.sum(p, axis=-1, keepdims=True)
    o = jnp.dot(p.astype(v_ref.dtype), v_ref[...], preferred_element_type=jnp.float32)
    o_ref[...] = (o / l).astype(o_ref.dtype)


def _mem_attention(z3, mem_kv3, *, tq):
    b, s, _ = z3.shape
    w = MEM_HEAD_DIM
    return pl.pallas_call(
        _mem_kernel,
        grid=(b, MEM_HEADS, s // tq),
        in_specs=[pl.BlockSpec((None, tq, w), lambda bi, h, i: (bi, i, Z_QC // w + h)),
                  pl.BlockSpec((None, N_MEM, w), lambda bi, h, i: (bi, 0, h)),
                  pl.BlockSpec((None, N_MEM, w), lambda bi, h, i: (bi, 0, MEM_HEADS + h))],
        out_specs=pl.BlockSpec((None, tq, w), lambda bi, h, i: (bi, i, h)),
        out_shape=jax.ShapeDtypeStruct((b, s, MEM_Q), jnp.bfloat16),
        compiler_params=_params(("parallel", "parallel", "parallel")),
        name="mem_attention",
    )(z3, mem_kv3, mem_kv3)


def _merge_kernel(x_ref, gate_ref, oa_ref, ob_ref, oc_ref, wb_ref, wo_ref, o_ref):
    merged = None
    for br, br_ref in enumerate((oa_ref, ob_ref, oc_ref)):
        proj = jnp.dot(br_ref[...], wb_ref[br], preferred_element_type=jnp.float32)
        gate = gate_ref[:, br * D_MODEL:(br + 1) * D_MODEL].astype(jnp.float32)
        term = gate * proj
        merged = term if merged is None else merged + term
    y = jnp.dot(merged.astype(wo_ref.dtype), wo_ref[...], preferred_element_type=jnp.float32)
    o_ref[...] = x_ref[...] + y


def _merge_out(x, z, o_a, o_b, o_c, w_branch, w_out, layer, *, tm):
    m, d = x.shape
    once = pl.Buffered(1)
    return pl.pallas_call(
        _merge_kernel,
        grid=(m // tm,),
        in_specs=[pl.BlockSpec((tm, d), lambda i: (i, 0)),
                  pl.BlockSpec((tm, GATE_W), lambda i: (i, Z_GATE // GATE_W)),
                  pl.BlockSpec((tm, BRANCH_WIDTH), lambda i: (i, 0)),
                  pl.BlockSpec((tm, BRANCH_WIDTH), lambda i: (i, 0)),
                  pl.BlockSpec((tm, BRANCH_WIDTH), lambda i: (i, 0)),
                  pl.BlockSpec((None, N_BRANCH, BRANCH_WIDTH, d), lambda i: (layer, 0, 0, 0),
                               pipeline_mode=once),
                  pl.BlockSpec((None, d, d), lambda i: (layer, 0, 0), pipeline_mode=once)],
        out_specs=pl.BlockSpec((tm, d), lambda i: (i, 0)),
        out_shape=jax.ShapeDtypeStruct((m, d), jnp.float32),
        compiler_params=_params(("parallel",)),
        name="merge_out",
    )(x, z, o_a, o_b, o_c, w_branch, w_out)


def _mlp_kernel(x_ref, g_ref, w1_ref, w2_ref, fg_ref, o_ref, h_ref, *, final_norm):
    j = pl.program_id(1)

    @pl.when(j == 0)
    def _():
        _store_rmsnorm(h_ref, x_ref, g_ref)
        o_ref[...] = x_ref[...]

    u = jnp.dot(h_ref[...], w1_ref[...], preferred_element_type=jnp.float32)
    u = jnp.square(jnp.maximum(u, 0.0)).astype(w2_ref.dtype)
    o_ref[...] += jnp.dot(u, w2_ref[...], preferred_element_type=jnp.float32)

    if final_norm:
        @pl.when(j == pl.num_programs(1) - 1)
        def _():
            y = o_ref[...]
            ms = jnp.mean(y * y, axis=-1, keepdims=True)
            o_ref[...] = y * lax.rsqrt(ms + RMS_EPS) * fg_ref[...]


def _mlp(x, g, w1, w2, final_g, layer, *, tm, tf, final_norm):
    m, d = x.shape
    f = w1.shape[-1]
    return pl.pallas_call(
        functools.partial(_mlp_kernel, final_norm=final_norm),
        grid=(m // tm, f // tf),
        in_specs=[pl.BlockSpec((tm, d), lambda i, j: (i, 0)),
                  pl.BlockSpec((None, 1, d), lambda i, j: (layer, 0, 0)),
                  pl.BlockSpec((None, d, tf), lambda i, j: (layer, 0, j)),
                  pl.BlockSpec((None, tf, d), lambda i, j: (layer, j, 0)),
                  pl.BlockSpec((1, d), lambda i, j: (0, 0))],
        out_specs=pl.BlockSpec((tm, d), lambda i, j: (i, 0)),
        out_shape=jax.ShapeDtypeStruct((m, d), jnp.float32),
        scratch_shapes=[pltpu.VMEM((tm, d), jnp.bfloat16)],
        compiler_params=_params(("parallel", "arbitrary")),
        name="mlp",
    )(x, g, w1, w2, final_g)


def kernel(x, mem, rel_bias, norm1_g, w_in, b_gate, swa_sinks, lam_q1, lam_k1, lam_q2, lam_k2,
           diff_subln_g, mem_norm_g, w_mem_kv, w_branch, w_out, norm2_g, w_mlp1, w_mlp2, final_g):
    bsz, seq, d = x.shape
    m = bsz * seq
    bf = jnp.bfloat16

    gate_col = w_in.shape[-1] - GATE_W
    w_in_z = jnp.concatenate([w_in[..., gate_col:], w_in[..., :gate_col]], axis=-1).astype(bf)
    w_mem_kv_b = w_mem_kv.astype(bf)
    w_branch_b = w_branch.astype(bf)
    w_out_b = w_out.astype(bf)
    w_mlp1_b = w_mlp1.astype(bf)
    w_mlp2_b = w_mlp2.astype(bf)

    norm1_g3 = norm1_g.reshape(DEPTH, 1, d)
    norm2_g3 = norm2_g.reshape(DEPTH, 1, d)
    mem_norm_g3 = mem_norm_g.reshape(DEPTH, 1, d)
    b_gate3 = b_gate.reshape(DEPTH, 1, GATE_W)
    subln_g3 = diff_subln_g.reshape(DEPTH, 1, DIFF_V_DIM)
    lam_vecs = jnp.stack([lam_q1, lam_k1, lam_q2, lam_k2], axis=1)
    final_g2 = final_g.reshape(1, d)
    col_scale = jnp.asarray(_z_column_scale())

    swa_bias = _bias_tiles(rel_bias, _swa_bucket_maps(), SWA_HEADS, 0, relative=False)
    diff_bias = _bias_tiles(rel_bias, _diff_bucket_maps(), DIFF_HEADS, SWA_HEADS, relative=True)

    x2 = x.reshape(m, d)
    mem2 = mem.reshape(bsz * N_MEM, d)
    for layer in range(DEPTH):
        z = _proj_in(x2, norm1_g3, w_in_z, b_gate3, col_scale, layer, tm=1024, tn=768)
        z3 = z.reshape(bsz, seq, D_IN)
        mem_kv = _norm_matmul(mem2, mem_norm_g3, w_mem_kv_b, layer, tm=1024, tn=1024)
        o_a = _swa_attention(z3, swa_sinks, swa_bias, layer)
        o_b = _diff_attention(z3, lam_vecs, subln_g3, diff_bias, layer)
        o_c = _mem_attention(z3, mem_kv.reshape(bsz, N_MEM, 2 * MEM_Q), tq=1024)
        x2 = _merge_out(x2, z, o_a.reshape(m, SWA_Q), o_b.reshape(m, DIFF_V), o_c.reshape(m, MEM_Q),
                        w_branch_b, w_out_b, layer, tm=256)
        x2 = _mlp(x2, norm2_g3, w_mlp1_b, w_mlp2_b, final_g2, layer, tm=1024, tf=512,
                  final_norm=(layer == DEPTH - 1))
    return x2.reshape(bsz, seq, d)
```

```python
import functools
import math

import numpy as np
import jax
import jax.numpy as jnp
from jax import lax
from jax.experimental import pallas as pl
from jax.experimental.pallas import tpu as pltpu

D_MODEL = 2048
DEPTH = 4
N_MEM = 256
RMS_EPS = 1e-6

SWA_WINDOW = 128
SWA_BLOCK = 128
SWA_HEADS = 16
SWA_KV_HEADS = 2
SWA_GROUP = SWA_HEADS // SWA_KV_HEADS
SWA_HEAD_DIM = 64
SWA_Q = SWA_HEADS * SWA_HEAD_DIM
SWA_KV = SWA_KV_HEADS * SWA_HEAD_DIM
SWA_STEP_BLOCKS = 4

DIFF_HEADS = 8
DIFF_HEAD_DIM = 64
DIFF_V_DIM = 2 * DIFF_HEAD_DIM
DIFF_QK = DIFF_HEADS * 2 * DIFF_HEAD_DIM
DIFF_V = DIFF_HEADS * DIFF_V_DIM
DIFF_SUBLN_EPS = 1e-5
DIFF_TILE = 256

MEM_HEADS = 4
MEM_HEAD_DIM = 256
MEM_Q = MEM_HEADS * MEM_HEAD_DIM

N_BRANCH = 3
BRANCH_WIDTH = 1024
GATE_W = N_BRANCH * D_MODEL
D_FF = 4 * D_MODEL

N_BUCKETS = 32
MAX_DISTANCE = 128
N_BIAS_HEADS = SWA_HEADS + DIFF_HEADS

Z_QA = 0
Z_KA = Z_QA + SWA_Q
Z_VA = Z_KA + SWA_KV
Z_QB = Z_VA + SWA_KV
Z_KB = Z_QB + DIFF_QK
Z_VB = Z_KB + DIFF_QK
Z_QC = Z_VB + DIFF_V
Z_GATE = Z_QC + MEM_Q
D_IN = Z_GATE + GATE_W

NEG = -1e30
LOG2E = math.log2(math.e)
BF16_SUBLANES = 16
PROJ_ROW_CHUNK = 256
VMEM_LIMIT_V7X = 56 * 1024 * 1024


def _params(sem, vmem=VMEM_LIMIT_V7X):
    return pltpu.CompilerParams(dimension_semantics=sem, vmem_limit_bytes=vmem)


def _z_column_scale():
    cs = np.ones((1, D_IN), np.float32)
    cs[0, Z_QA:Z_QA + SWA_Q] = SWA_HEAD_DIM ** -0.5 * LOG2E
    cs[0, Z_QB:Z_QB + DIFF_QK] = DIFF_HEAD_DIM ** -0.5 * LOG2E
    cs[0, Z_QC:Z_QC + MEM_Q] = MEM_HEAD_DIM ** -0.5 * LOG2E
    return cs


def _column_max(s):
    r, c = s.shape
    return jnp.max(jnp.max(s.reshape(r // 8, 8, c), axis=0), axis=0, keepdims=True)


def _t5_bucket_np(dist):
    max_exact = N_BUCKETS // 2
    n = np.maximum(dist, 0)
    nf = np.maximum(n, 1).astype(np.float64)
    large = max_exact + (np.log(nf / max_exact) / math.log(MAX_DISTANCE / max_exact)
                         * (N_BUCKETS - max_exact)).astype(np.int32)
    large = np.minimum(large, N_BUCKETS - 1)
    return np.where(n < max_exact, n, large).astype(np.int32)


def _swa_bucket_maps():
    kj = np.arange(2 * SWA_BLOCK)[:, None]
    qi = np.arange(SWA_BLOCK)[None, :]
    dist = qi + SWA_BLOCK - kj
    valid = (dist >= 0) & (dist < SWA_WINDOW)
    b = np.where(valid, _t5_bucket_np(dist), -1)
    first = np.where(kj >= SWA_BLOCK, b, -1)
    return np.stack([b, first]).astype(np.int32)


def _diff_bucket_maps():
    t = DIFF_TILE
    kj = np.arange(t)[:, None]
    qi = np.arange(t)[None, :]
    prev = _t5_bucket_np(qi + t - kj)
    d = qi - kj
    diag = np.where(d >= 0, _t5_bucket_np(d), -1)
    return np.stack([prev, diag]).astype(np.int32)


def _bias_tiles_kernel(tab_ref, bucket_ref, o_ref, *, head_offset, relative):
    h = pl.program_id(0) + head_offset
    bucket = bucket_ref[...]
    base = tab_ref[N_BUCKETS - 1, h] if relative else 0.0
    acc = jnp.full(bucket.shape, NEG, jnp.float32)
    for b in range(N_BUCKETS):
        acc = jnp.where(bucket == b, (tab_ref[b, h] - base) * LOG2E, acc)
    o_ref[...] = acc


def _bias_tiles(rel_bias, bucket_maps, n_heads, head_offset, relative):
    v, r, c = bucket_maps.shape
    return pl.pallas_call(
        functools.partial(_bias_tiles_kernel, head_offset=head_offset, relative=relative),
        grid=(n_heads,),
        in_specs=[pl.BlockSpec(memory_space=pltpu.SMEM),
                  pl.BlockSpec((v, r, c), lambda h: (0, 0, 0))],
        out_specs=pl.BlockSpec((None, v, r, c), lambda h: (h, 0, 0, 0)),
        out_shape=jax.ShapeDtypeStruct((n_heads, v, r, c), jnp.float32),
        compiler_params=_params(("parallel",)),
        name="bias_tiles",
    )(rel_bias, jnp.asarray(bucket_maps))


def _store_rmsnorm(h_ref, x_ref, g_ref):
    xf = x_ref[...]
    ms = jnp.mean(xf * xf, axis=-1, keepdims=True)
    h_ref[...] = (xf * lax.rsqrt(ms + RMS_EPS) * g_ref[...]).astype(h_ref.dtype)


def _proj_in_kernel(x_ref, g_ref, w_ref, b_ref, cs_ref, o_ref, h_ref, *, first_gate_block):
    j = pl.program_id(1)

    @pl.when(j == 0)
    def _():
        _store_rmsnorm(h_ref, x_ref, g_ref)

    is_gate = j >= first_gate_block
    for r in range(0, h_ref.shape[0], PROJ_ROW_CHUNK):
        rows = slice(r, r + PROJ_ROW_CHUNK)
        acc = jnp.dot(h_ref[rows, :], w_ref[...], preferred_element_type=jnp.float32)
        gate = 0.5 * jnp.tanh(0.5 * (acc + b_ref[...])) + 0.5
        o_ref[rows, :] = jnp.where(is_gate, gate, acc * cs_ref[...]).astype(o_ref.dtype)


def _proj_in(x, g, w, b, col_scale, layer, *, tm, tn):
    m, k = x.shape
    n = w.shape[-1]
    first_gate_block = Z_GATE // tn
    return pl.pallas_call(
        functools.partial(_proj_in_kernel, first_gate_block=first_gate_block),
        grid=(m // tm, n // tn),
        in_specs=[pl.BlockSpec((tm, k), lambda i, j: (i, 0)),
                  pl.BlockSpec((None, 1, k), lambda i, j: (layer, 0, 0)),
                  pl.BlockSpec((None, k, tn), lambda i, j: (layer, 0, j)),
                  pl.BlockSpec((None, 1, tn), lambda i, j: (layer, 0, jnp.maximum(j - first_gate_block, 0))),
                  pl.BlockSpec((1, tn), lambda i, j: (0, j))],
        out_specs=pl.BlockSpec((tm, tn), lambda i, j: (i, j)),
        out_shape=jax.ShapeDtypeStruct((m, n), jnp.bfloat16),
        scratch_shapes=[pltpu.VMEM((tm, k), jnp.bfloat16)],
        compiler_params=_params(("parallel", "arbitrary")),
        name="proj_in",
    )(x, g, w, b, col_scale)


def _norm_matmul_kernel(x_ref, g_ref, w_ref, o_ref, h_ref):
    @pl.when(pl.program_id(1) == 0)
    def _():
        _store_rmsnorm(h_ref, x_ref, g_ref)

    o_ref[...] = jnp.dot(h_ref[...], w_ref[...], preferred_element_type=jnp.float32).astype(o_ref.dtype)


def _norm_matmul(x, g, w, layer, *, tm, tn):
    m, k = x.shape
    n = w.shape[-1]
    return pl.pallas_call(
        _norm_matmul_kernel,
        grid=(m // tm, n // tn),
        in_specs=[pl.BlockSpec((tm, k), lambda i, j: (i, 0)),
                  pl.BlockSpec((None, 1, k), lambda i, j: (layer, 0, 0)),
                  pl.BlockSpec((None, k, tn), lambda i, j: (layer, 0, j))],
        out_specs=pl.BlockSpec((tm, tn), lambda i, j: (i, j)),
        out_shape=jax.ShapeDtypeStruct((m, n), jnp.bfloat16),
        scratch_shapes=[pltpu.VMEM((tm, k), jnp.bfloat16)],
        compiler_params=_params(("parallel", "arbitrary")),
        name="norm_matmul",
    )(x, g, w)


def _swa_kernel(sink_ref, q_ref, kp_ref, kc_ref, vp_ref, vc_ref, bias_ref, o_ref, *, layer):
    hd, blk = SWA_HEAD_DIM, SWA_BLOCK
    k = jnp.concatenate([kp_ref[...], kc_ref[...]], axis=0)
    v = jnp.concatenate([vp_ref[...], vc_ref[...]], axis=0)
    v_t = v.astype(jnp.float32).T.astype(v.dtype)
    ones = jnp.ones((BF16_SUBLANES, 2 * blk), v.dtype)
    for u in range(SWA_STEP_BLOCKS):
        first = (pl.program_id(1) == 0).astype(jnp.int32) if u == 0 else 0
        q = q_ref[u * blk:(u + 1) * blk, :]
        k_win = k[u * blk:(u + 2) * blk]
        outs = []
        for hk in range(SWA_KV_HEADS):
            heads = range(hk * SWA_GROUP, (hk + 1) * SWA_GROUP)
            qs = jnp.concatenate([q[:, h * hd:(h + 1) * hd] for h in heads], axis=0)
            s = lax.dot_general(k_win[:, hk * hd:(hk + 1) * hd], qs, (((1,), (1,)), ((), ())),
                                preferred_element_type=jnp.float32)
            s = s + jnp.concatenate([bias_ref[h, first] for h in heads], axis=1)
            sink = jnp.concatenate([jnp.full((1, blk), sink_ref[layer, h] * LOG2E, jnp.float32)
                                    for h in heads], axis=1)
            m = jnp.maximum(_column_max(s), sink)
            p = jnp.exp2(s - m).astype(v.dtype)
            v_ext = jnp.concatenate([v_t[hk * hd:(hk + 1) * hd, u * blk:(u + 2) * blk], ones], axis=0)
            o_ext = jnp.dot(v_ext, p, preferred_element_type=jnp.float32)
            denom = o_ext[hd:hd + 1] + jnp.exp2(sink - m)
            o_t = o_ext[:hd] / denom
            outs += [o_t[:, g * blk:(g + 1) * blk].T for g in range(SWA_GROUP)]
        o_ref[u * blk:(u + 1) * blk, :] = jnp.concatenate(outs, axis=1).astype(o_ref.dtype)


def _swa_attention(z3, sinks, bias, layer):
    b, s, _ = z3.shape
    rows = SWA_STEP_BLOCKS * SWA_BLOCK
    kcol, vcol = Z_KA // SWA_KV, Z_VA // SWA_KV
    prev = lambda i: jnp.maximum(i * SWA_STEP_BLOCKS - 1, 0)
    return pl.pallas_call(
        functools.partial(_swa_kernel, layer=layer),
        grid=(b, s // rows),
        in_specs=[pl.BlockSpec(memory_space=pltpu.SMEM),
                  pl.BlockSpec((None, rows, SWA_Q), lambda bi, i: (bi, i, Z_QA // SWA_Q)),
                  pl.BlockSpec((None, SWA_BLOCK, SWA_KV), lambda bi, i: (bi, prev(i), kcol)),
                  pl.BlockSpec((None, rows, SWA_KV), lambda bi, i: (bi, i, kcol)),
                  pl.BlockSpec((None, SWA_BLOCK, SWA_KV), lambda bi, i: (bi, prev(i), vcol)),
                  pl.BlockSpec((None, rows, SWA_KV), lambda bi, i: (bi, i, vcol)),
                  pl.BlockSpec(bias.shape, lambda bi, i: (0, 0, 0, 0), pipeline_mode=pl.Buffered(1))],
        out_specs=pl.BlockSpec((None, rows, SWA_Q), lambda bi, i: (bi, i, 0)),
        out_shape=jax.ShapeDtypeStruct((b, s, SWA_Q), jnp.bfloat16),
        compiler_params=_params(("parallel", "parallel")),
        name="swa_attention",
    )(sinks, z3, z3, z3, z3, z3, bias)


def _diff_kernel(lam_ref, g_ref, q_ref, k_ref, v_ref, bias_ref, o_ref, *, layer):
    t = DIFF_TILE
    seq = q_ref.shape[0]
    lam_init = 0.8 - 0.6 * math.exp(-0.3 * layer)
    lv = lam_ref[...]
    lam = (jnp.exp(jnp.sum(lv[0:1] * lv[1:2], axis=-1, keepdims=True))
           - jnp.exp(jnp.sum(lv[2:3] * lv[3:4], axis=-1, keepdims=True)) + lam_init)

    v_t = v_ref[...].astype(jnp.float32).T.astype(v_ref.dtype)
    ones = jnp.ones((BF16_SUBLANES, seq), v_ref.dtype)
    v_ext = jnp.concatenate([v_t, ones], axis=0)
    prev2 = jnp.concatenate([bias_ref[0], bias_ref[0]], axis=1)
    diag2 = jnp.concatenate([bias_ref[1], bias_ref[1]], axis=1)
    lane = lax.broadcasted_iota(jnp.int32, (t, 2 * DIFF_HEAD_DIM), 1)
    n_blocks = seq // t

    @functools.cache
    def stacked_queries(i):
        q = q_ref[i * t:(i + 1) * t, :]
        zero = jnp.zeros_like(q)
        return jnp.concatenate([jnp.where(lane < DIFF_HEAD_DIM, q, zero),
                                jnp.where(lane >= DIFF_HEAD_DIM, q, zero)], axis=0)

    def score_chunk(i, c):
        s = lax.dot_general(k_ref[c * t:(c + 1) * t, :], stacked_queries(i), (((1,), (1,)), ((), ())),
                            preferred_element_type=jnp.float32)
        return s + diag2 if c == i else s + prev2 if c == i - 1 else s

    order = list(range(0, n_blocks, 2)) + list(range(n_blocks - 1 - n_blocks % 2, 0, -2))
    chunks = [score_chunk(order[0], c) for c in range(order[0] + 1)]
    for pos, i in enumerate(order):
        nxt = order[pos + 1] if pos + 1 < n_blocks else None
        m = functools.reduce(jnp.maximum, [_column_max(s) for s in chunks])
        acc = None
        next_chunks = []
        for c in range(max(i + 1, 0 if nxt is None else nxt + 1)):
            if nxt is not None and c <= nxt:
                next_chunks.append(score_chunk(nxt, c))
            if c <= i:
                p = jnp.exp2(chunks[c] - m).astype(v_ref.dtype)
                pv = jnp.dot(v_ext[:, c * t:(c + 1) * t], p, preferred_element_type=jnp.float32)
                acc = pv if acc is None else acc + pv
        chunks = next_chunks
        o2_t = acc[:DIFF_V_DIM] / acc[DIFF_V_DIM:DIFF_V_DIM + 1]
        o = (o2_t[:, :t] - lam * o2_t[:, t:]).T
        ms = jnp.mean(o * o, axis=-1, keepdims=True)
        o = o * lax.rsqrt(ms + DIFF_SUBLN_EPS) * g_ref[...]
        o_ref[i * t:(i + 1) * t, :] = (o * (1.0 - lam_init)).astype(o_ref.dtype)


def _diff_attention(z3, lam_vecs, subln_g, bias, layer):
    b, s, _ = z3.shape
    t = DIFF_TILE
    w = DIFF_V_DIM
    return pl.pallas_call(
        functools.partial(_diff_kernel, layer=layer),
        grid=(b, DIFF_HEADS),
        in_specs=[pl.BlockSpec((None, 4, DIFF_HEAD_DIM), lambda bi, h: (layer, 0, 0)),
                  pl.BlockSpec((None, 1, w), lambda bi, h: (layer, 0, 0)),
                  pl.BlockSpec((None, s, w), lambda bi, h: (bi, 0, Z_QB // w + h)),
                  pl.BlockSpec((None, s, w), lambda bi, h: (bi, 0, Z_KB // w + h)),
                  pl.BlockSpec((None, s, w), lambda bi, h: (bi, 0, Z_VB // w + h)),
                  pl.BlockSpec((None, 2, t, t), lambda bi, h: (h, 0, 0, 0))],
        out_specs=pl.BlockSpec((None, s, w), lambda bi, h: (bi, 0, h)),
        out_shape=jax.ShapeDtypeStruct((b, s, DIFF_V), jnp.bfloat16),
        compiler_params=_params(("parallel", "parallel")),
        name="diff_attention",
    )(lam_vecs, subln_g, z3, z3, z3, bias)


def _mem_kernel(q_ref, k_ref, v_ref, o_ref):
    s = lax.dot_general(q_ref[...], k_ref[...], (((1,), (1,)), ((), ())),
                        preferred_element_type=jnp.float32)
    m = jnp.max(s, axis=-1, keepdims=True)
    p = jnp.exp2(s - m)
    l = jnp.sum(p, axis=-1, keepdims=True)
    o = jnp.dot(p.astype(v_ref.dtype), v_ref[...], preferred_element_type=jnp.float32)
    o_ref[...] = (o / l).astype(o_ref.dtype)


def _mem_attention(z3, mem_kv3, *, tq):
    b, s, _ = z3.shape
    w = MEM_HEAD_DIM
    return pl.pallas_call(
        _mem_kernel,
        grid=(b, MEM_HEADS, s // tq),
        in_specs=[pl.BlockSpec((None, tq, w), lambda bi, h, i: (bi, i, Z_QC // w + h)),
                  pl.BlockSpec((None, N_MEM, w), lambda bi, h, i: (bi, 0, h)),
                  pl.BlockSpec((None, N_MEM, w), lambda bi, h, i: (bi, 0, MEM_HEADS + h))],
        out_specs=pl.BlockSpec((None, tq, w), lambda bi, h, i: (bi, i, h)),
        out_shape=jax.ShapeDtypeStruct((b, s, MEM_Q), jnp.bfloat16),
        compiler_params=_params(("parallel", "parallel", "parallel")),
        name="mem_attention",
    )(z3, mem_kv3, mem_kv3)


def _merge_kernel(x_ref, gate_ref, oa_ref, ob_ref, oc_ref, wb_ref, wo_ref, o_ref):
    merged = None
    for br, br_ref in enumerate((oa_ref, ob_ref, oc_ref)):
        proj = jnp.dot(br_ref[...], wb_ref[br], preferred_element_type=jnp.float32)
        gate = gate_ref[:, br * D_MODEL:(br + 1) * D_MODEL].astype(jnp.float32)
        term = gate * proj
        merged = term if merged is None else merged + term
    y = jnp.dot(merged.astype(wo_ref.dtype), wo_ref[...], preferred_element_type=jnp.float32)
    o_ref[...] = x_ref[...] + y


def _merge_out(x, z, o_a, o_b, o_c, w_branch, w_out, layer, *, tm):
    m, d = x.shape
    once = pl.Buffered(1)
    return pl.pallas_call(
        _merge_kernel,
        grid=(m // tm,),
        in_specs=[pl.BlockSpec((tm, d), lambda i: (i, 0)),
                  pl.BlockSpec((pl.Element(tm), pl.Element(GATE_W)), lambda i: (i * tm, Z_GATE)),
                  pl.BlockSpec((tm, BRANCH_WIDTH), lambda i: (i, 0)),
                  pl.BlockSpec((tm, BRANCH_WIDTH), lambda i: (i, 0)),
                  pl.BlockSpec((tm, BRANCH_WIDTH), lambda i: (i, 0)),
                  pl.BlockSpec((None, N_BRANCH, BRANCH_WIDTH, d), lambda i: (layer, 0, 0, 0),
                               pipeline_mode=once),
                  pl.BlockSpec((None, d, d), lambda i: (layer, 0, 0), pipeline_mode=once)],
        out_specs=pl.BlockSpec((tm, d), lambda i: (i, 0)),
        out_shape=jax.ShapeDtypeStruct((m, d), jnp.float32),
        compiler_params=_params(("parallel",)),
        name="merge_out",
    )(x, z, o_a, o_b, o_c, w_branch, w_out)


def _mlp_kernel(x_ref, g_ref, w1_ref, w2_ref, fg_ref, o_ref, h_ref, *, final_norm):
    j = pl.program_id(1)

    @pl.when(j == 0)
    def _():
        _store_rmsnorm(h_ref, x_ref, g_ref)
        o_ref[...] = x_ref[...]

    u = jnp.dot(h_ref[...], w1_ref[...], preferred_element_type=jnp.float32)
    u = jnp.square(jnp.maximum(u, 0.0)).astype(w2_ref.dtype)
    o_ref[...] += jnp.dot(u, w2_ref[...], preferred_element_type=jnp.float32)

    if final_norm:
        @pl.when(j == pl.num_programs(1) - 1)
        def _():
            y = o_ref[...]
            ms = jnp.mean(y * y, axis=-1, keepdims=True)
            o_ref[...] = y * lax.rsqrt(ms + RMS_EPS) * fg_ref[...]


def _mlp(x, g, w1, w2, final_g, layer, *, tm, tf, final_norm):
    m, d = x.shape
    f = w1.shape[-1]
    return pl.pallas_call(
        functools.partial(_mlp_kernel, final_norm=final_norm),
        grid=(m // tm, f // tf),
        in_specs=[pl.BlockSpec((tm, d), lambda i, j: (i, 0)),
                  pl.BlockSpec((None, 1, d), lambda i, j: (layer, 0, 0)),
                  pl.BlockSpec((None, d, tf), lambda i, j: (layer, 0, j)),
                  pl.BlockSpec((None, tf, d), lambda i, j: (layer, j, 0)),
                  pl.BlockSpec((1, d), lambda i, j: (0, 0))],
        out_specs=pl.BlockSpec((tm, d), lambda i, j: (i, 0)),
        out_shape=jax.ShapeDtypeStruct((m, d), jnp.float32),
        scratch_shapes=[pltpu.VMEM((tm, d), jnp.bfloat16)],
        compiler_params=_params(("parallel", "arbitrary")),
        name="mlp",
    )(x, g, w1, w2, final_g)


def kernel(x, mem, rel_bias, norm1_g, w_in, b_gate, swa_sinks, lam_q1, lam_k1, lam_q2, lam_k2,
           diff_subln_g, mem_norm_g, w_mem_kv, w_branch, w_out, norm2_g, w_mlp1, w_mlp2, final_g):
    bsz, seq, d = x.shape
    m = bsz * seq
    bf = jnp.bfloat16

    w_in_z = w_in.astype(bf)
    w_mem_kv_b = w_mem_kv.astype(bf)
    w_branch_b = w_branch.astype(bf)
    w_out_b = w_out.astype(bf)
    w_mlp1_b = w_mlp1.astype(bf)
    w_mlp2_b = w_mlp2.astype(bf)

    norm1_g3 = norm1_g.reshape(DEPTH, 1, d)
    norm2_g3 = norm2_g.reshape(DEPTH, 1, d)
    mem_norm_g3 = mem_norm_g.reshape(DEPTH, 1, d)
    b_gate3 = b_gate.reshape(DEPTH, 1, GATE_W)
    subln_g3 = diff_subln_g.reshape(DEPTH, 1, DIFF_V_DIM)
    lam_vecs = jnp.stack([lam_q1, lam_k1, lam_q2, lam_k2], axis=1)
    final_g2 = final_g.reshape(1, d)
    col_scale = jnp.asarray(_z_column_scale())

    swa_bias = _bias_tiles(rel_bias, _swa_bucket_maps(), SWA_HEADS, 0, relative=False)
    diff_bias = _bias_tiles(rel_bias, _diff_bucket_maps(), DIFF_HEADS, SWA_HEADS, relative=True)

    x2 = x.reshape(m, d)
    mem2 = mem.reshape(bsz * N_MEM, d)
    for layer in range(DEPTH):
        z = _proj_in(x2, norm1_g3, w_in_z, b_gate3, col_scale, layer, tm=1024, tn=768)
        z3 = z.reshape(bsz, seq, D_IN)
        mem_kv = _norm_matmul(mem2, mem_norm_g3, w_mem_kv_b, layer, tm=1024, tn=1024)
        o_a = _swa_attention(z3, swa_sinks, swa_bias, layer)
        o_b = _diff_attention(z3, lam_vecs, subln_g3, diff_bias, layer)
        o_c = _mem_attention(z3, mem_kv.reshape(bsz, N_MEM, 2 * MEM_Q), tq=1024)
        x2 = _merge_out(x2, z, o_a.reshape(m, SWA_Q), o_b.reshape(m, DIFF_V), o_c.reshape(m, MEM_Q),
                        w_branch_b, w_out_b, layer, tm=256)
        x2 = _mlp(x2, norm2_g3, w_mlp1_b, w_mlp2_b, final_g2, layer, tm=1024, tf=512,
                  final_norm=(layer == DEPTH - 1))
    return x2.reshape(bsz, seq, d)
```

```python
import functools
import math

import numpy as np
import jax
import jax.numpy as jnp
from jax import lax
from jax.experimental import pallas as pl
from jax.experimental.pallas import tpu as pltpu

D_MODEL = 2048
DEPTH = 4
N_MEM = 256
RMS_EPS = 1e-6

SWA_WINDOW = 128
SWA_BLOCK = 128
SWA_HEADS = 16
SWA_KV_HEADS = 2
SWA_GROUP = SWA_HEADS // SWA_KV_HEADS
SWA_HEAD_DIM = 64
SWA_Q = SWA_HEADS * SWA_HEAD_DIM
SWA_KV = SWA_KV_HEADS * SWA_HEAD_DIM
SWA_STEP_BLOCKS = 4

DIFF_HEADS = 8
DIFF_HEAD_DIM = 64
DIFF_V_DIM = 2 * DIFF_HEAD_DIM
DIFF_QK = DIFF_HEADS * 2 * DIFF_HEAD_DIM
DIFF_V = DIFF_HEADS * DIFF_V_DIM
DIFF_SUBLN_EPS = 1e-5
DIFF_TILE = 256
DIFF_STEP_HEADS = 2

MEM_HEADS = 4
MEM_HEAD_DIM = 256
MEM_Q = MEM_HEADS * MEM_HEAD_DIM

N_BRANCH = 3
BRANCH_WIDTH = 1024
GATE_W = N_BRANCH * D_MODEL
D_FF = 4 * D_MODEL

N_BUCKETS = 32
MAX_DISTANCE = 128
N_BIAS_HEADS = SWA_HEADS + DIFF_HEADS

Z_QA = 0
Z_KA = Z_QA + SWA_Q
Z_VA = Z_KA + SWA_KV
Z_QB = Z_VA + SWA_KV
Z_KB = Z_QB + DIFF_QK
Z_VB = Z_KB + DIFF_QK
Z_QC = Z_VB + DIFF_V
Z_GATE = Z_QC + MEM_Q
D_IN = Z_GATE + GATE_W

NEG = -1e30
LOG2E = math.log2(math.e)
BF16_SUBLANES = 16
PROJ_ROW_CHUNK = 256
VMEM_LIMIT_V7X = 56 * 1024 * 1024


def _params(sem, vmem=VMEM_LIMIT_V7X):
    return pltpu.CompilerParams(dimension_semantics=sem, vmem_limit_bytes=vmem)


def _z_column_scale():
    cs = np.ones((1, D_IN), np.float32)
    cs[0, Z_QA:Z_QA + SWA_Q] = SWA_HEAD_DIM ** -0.5 * LOG2E
    cs[0, Z_QB:Z_QB + DIFF_QK] = DIFF_HEAD_DIM ** -0.5 * LOG2E
    cs[0, Z_QC:Z_QC + MEM_Q] = MEM_HEAD_DIM ** -0.5 * LOG2E
    return cs


def _column_max(s):
    r, c = s.shape
    return jnp.max(jnp.max(s.reshape(r // 8, 8, c), axis=0), axis=0, keepdims=True)


def _t5_bucket_np(dist):
    max_exact = N_BUCKETS // 2
    n = np.maximum(dist, 0)
    nf = np.maximum(n, 1).astype(np.float64)
    large = max_exact + (np.log(nf / max_exact) / math.log(MAX_DISTANCE / max_exact)
                         * (N_BUCKETS - max_exact)).astype(np.int32)
    large = np.minimum(large, N_BUCKETS - 1)
    return np.where(n < max_exact, n, large).astype(np.int32)


def _swa_bucket_maps():
    kj = np.arange(2 * SWA_BLOCK)[:, None]
    qi = np.arange(SWA_BLOCK)[None, :]
    dist = qi + SWA_BLOCK - kj
    valid = (dist >= 0) & (dist < SWA_WINDOW)
    b = np.where(valid, _t5_bucket_np(dist), -1)
    first = np.where(kj >= SWA_BLOCK, b, -1)
    return np.stack([b, first]).astype(np.int32)


def _diff_bucket_maps():
    t = DIFF_TILE
    kj = np.arange(t)[:, None]
    qi = np.arange(t)[None, :]
    prev = _t5_bucket_np(qi + t - kj)
    d = qi - kj
    diag = np.where(d >= 0, _t5_bucket_np(d), -1)
    return np.stack([prev, diag]).astype(np.int32)


def _bias_tiles_kernel(tab_ref, bucket_ref, o_ref, *, head_offset, relative):
    h = pl.program_id(0) + head_offset
    bucket = bucket_ref[...]
    base = tab_ref[N_BUCKETS - 1, h] if relative else 0.0
    acc = jnp.full(bucket.shape, NEG, jnp.float32)
    for b in range(N_BUCKETS):
        acc = jnp.where(bucket == b, (tab_ref[b, h] - base) * LOG2E, acc)
    o_ref[...] = acc


def _bias_tiles(rel_bias, bucket_maps, n_heads, head_offset, relative):
    v, r, c = bucket_maps.shape
    return pl.pallas_call(
        functools.partial(_bias_tiles_kernel, head_offset=head_offset, relative=relative),
        grid=(n_heads,),
        in_specs=[pl.BlockSpec(memory_space=pltpu.SMEM),
                  pl.BlockSpec((v, r, c), lambda h: (0, 0, 0))],
        out_specs=pl.BlockSpec((None, v, r, c), lambda h: (h, 0, 0, 0)),
        out_shape=jax.ShapeDtypeStruct((n_heads, v, r, c), jnp.float32),
        compiler_params=_params(("parallel",)),
        name="bias_tiles",
    )(rel_bias, jnp.asarray(bucket_maps))


def _store_rmsnorm(h_ref, x_ref, g_ref):
    xf = x_ref[...]
    ms = jnp.mean(xf * xf, axis=-1, keepdims=True)
    h_ref[...] = (xf * lax.rsqrt(ms + RMS_EPS) * g_ref[...]).astype(h_ref.dtype)


def _proj_in_kernel(x_ref, g_ref, w_ref, b_ref, cs_ref, o_ref, h_ref, *, first_gate_block):
    j = pl.program_id(1)

    @pl.when(j == 0)
    def _():
        _store_rmsnorm(h_ref, x_ref, g_ref)

    is_gate = j >= first_gate_block
    for r in range(0, h_ref.shape[0], PROJ_ROW_CHUNK):
        rows = slice(r, r + PROJ_ROW_CHUNK)
        acc = jnp.dot(h_ref[rows, :], w_ref[...], preferred_element_type=jnp.float32)
        gate = 0.5 * jnp.tanh(0.5 * (acc + b_ref[...])) + 0.5
        o_ref[rows, :] = jnp.where(is_gate, gate, acc * cs_ref[...]).astype(o_ref.dtype)


def _proj_in(x, g, w, b, col_scale, layer, *, tm, tn):
    m, k = x.shape
    n = w.shape[-1]
    first_gate_block = Z_GATE // tn
    return pl.pallas_call(
        functools.partial(_proj_in_kernel, first_gate_block=first_gate_block),
        grid=(m // tm, n // tn),
        in_specs=[pl.BlockSpec((tm, k), lambda i, j: (i, 0)),
                  pl.BlockSpec((None, 1, k), lambda i, j: (layer, 0, 0)),
                  pl.BlockSpec((None, k, tn), lambda i, j: (layer, 0, j)),
                  pl.BlockSpec((None, 1, tn), lambda i, j: (layer, 0, jnp.maximum(j - first_gate_block, 0))),
                  pl.BlockSpec((1, tn), lambda i, j: (0, j))],
        out_specs=pl.BlockSpec((tm, tn), lambda i, j: (i, j)),
        out_shape=jax.ShapeDtypeStruct((m, n), jnp.bfloat16),
        scratch_shapes=[pltpu.VMEM((tm, k), jnp.bfloat16)],
        compiler_params=_params(("parallel", "arbitrary")),
        name="proj_in",
    )(x, g, w, b, col_scale)


def _norm_matmul_kernel(x_ref, g_ref, w_ref, o_ref, h_ref):
    @pl.when(pl.program_id(1) == 0)
    def _():
        _store_rmsnorm(h_ref, x_ref, g_ref)

    o_ref[...] = jnp.dot(h_ref[...], w_ref[...], preferred_element_type=jnp.float32).astype(o_ref.dtype)


def _norm_matmul(x, g, w, layer, *, tm, tn):
    m, k = x.shape
    n = w.shape[-1]
    return pl.pallas_call(
        _norm_matmul_kernel,
        grid=(m // tm, n // tn),
        in_specs=[pl.BlockSpec((tm, k), lambda i, j: (i, 0)),
                  pl.BlockSpec((None, 1, k), lambda i, j: (layer, 0, 0)),
                  pl.BlockSpec((None, k, tn), lambda i, j: (layer, 0, j))],
        out_specs=pl.BlockSpec((tm, tn), lambda i, j: (i, j)),
        out_shape=jax.ShapeDtypeStruct((m, n), jnp.bfloat16),
        scratch_shapes=[pltpu.VMEM((tm, k), jnp.bfloat16)],
        compiler_params=_params(("parallel", "arbitrary")),
        name="norm_matmul",
    )(x, g, w)


def _swa_kernel(sink_ref, q_ref, kp_ref, kc_ref, vp_ref, vc_ref, bias_ref, o_ref, *, layer):
    hd, blk = SWA_HEAD_DIM, SWA_BLOCK
    k = jnp.concatenate([kp_ref[...], kc_ref[...]], axis=0)
    v = jnp.concatenate([vp_ref[...], vc_ref[...]], axis=0)
    v_t = v.astype(jnp.float32).T.astype(v.dtype)
    ones = jnp.ones((BF16_SUBLANES, 2 * blk), v.dtype)
    for u in range(SWA_STEP_BLOCKS):
        first = (pl.program_id(1) == 0).astype(jnp.int32) if u == 0 else 0
        q = q_ref[u * blk:(u + 1) * blk, :]
        k_win = k[u * blk:(u + 2) * blk]
        outs = []
        for hk in range(SWA_KV_HEADS):
            heads = range(hk * SWA_GROUP, (hk + 1) * SWA_GROUP)
            qs = jnp.concatenate([q[:, h * hd:(h + 1) * hd] for h in heads], axis=0)
            s = lax.dot_general(k_win[:, hk * hd:(hk + 1) * hd], qs, (((1,), (1,)), ((), ())),
                                preferred_element_type=jnp.float32)
            s = s + jnp.concatenate([bias_ref[h, first] for h in heads], axis=1)
            sink = jnp.concatenate([jnp.full((1, blk), sink_ref[layer, h] * LOG2E, jnp.float32)
                                    for h in heads], axis=1)
            m = jnp.maximum(_column_max(s), sink)
            p = jnp.exp2(s - m).astype(v.dtype)
            v_ext = jnp.concatenate([v_t[hk * hd:(hk + 1) * hd, u * blk:(u + 2) * blk], ones], axis=0)
            o_ext = jnp.dot(v_ext, p, preferred_element_type=jnp.float32)
            denom = o_ext[hd:hd + 1] + jnp.exp2(sink - m)
            o_t = o_ext[:hd] / denom
            outs += [o_t[:, g * blk:(g + 1) * blk].T for g in range(SWA_GROUP)]
        o_ref[u * blk:(u + 1) * blk, :] = jnp.concatenate(outs, axis=1).astype(o_ref.dtype)


def _swa_attention(z3, sinks, bias, layer):
    b, s, _ = z3.shape
    rows = SWA_STEP_BLOCKS * SWA_BLOCK
    kcol, vcol = Z_KA // SWA_KV, Z_VA // SWA_KV
    prev = lambda i: jnp.maximum(i * SWA_STEP_BLOCKS - 1, 0)
    return pl.pallas_call(
        functools.partial(_swa_kernel, layer=layer),
        grid=(b, s // rows),
        in_specs=[pl.BlockSpec(memory_space=pltpu.SMEM),
                  pl.BlockSpec((None, rows, SWA_Q), lambda bi, i: (bi, i, Z_QA // SWA_Q)),
                  pl.BlockSpec((None, SWA_BLOCK, SWA_KV), lambda bi, i: (bi, prev(i), kcol)),
                  pl.BlockSpec((None, rows, SWA_KV), lambda bi, i: (bi, i, kcol)),
                  pl.BlockSpec((None, SWA_BLOCK, SWA_KV), lambda bi, i: (bi, prev(i), vcol)),
                  pl.BlockSpec((None, rows, SWA_KV), lambda bi, i: (bi, i, vcol)),
                  pl.BlockSpec(bias.shape, lambda bi, i: (0, 0, 0, 0), pipeline_mode=pl.Buffered(1))],
        out_specs=pl.BlockSpec((None, rows, SWA_Q), lambda bi, i: (bi, i, 0)),
        out_shape=jax.ShapeDtypeStruct((b, s, SWA_Q), jnp.bfloat16),
        compiler_params=_params(("parallel", "parallel")),
        name="swa_attention",
    )(sinks, z3, z3, z3, z3, z3, bias)


def _diff_kernel(lam_ref, g_ref, q_ref, k_ref, v_ref, bias_ref, o_ref, *, layer):
    t = DIFF_TILE
    w = DIFF_V_DIM
    seq = q_ref.shape[0]
    n_blocks = seq // t
    lam_init = 0.8 - 0.6 * math.exp(-0.3 * layer)
    lv = lam_ref[...]
    lam = (jnp.exp(jnp.sum(lv[0:1] * lv[1:2], axis=-1, keepdims=True))
           - jnp.exp(jnp.sum(lv[2:3] * lv[3:4], axis=-1, keepdims=True)) + lam_init)
    ones = jnp.ones((BF16_SUBLANES, seq), v_ref.dtype)
    lane = lax.broadcasted_iota(jnp.int32, (t, 2 * DIFF_HEAD_DIM), 1)

    @functools.cache
    def values_t(u):
        v_t = v_ref[:, u * w:(u + 1) * w].astype(jnp.float32).T.astype(v_ref.dtype)
        return jnp.concatenate([v_t, ones], axis=0)

    @functools.cache
    def bias2(u, tile):
        return jnp.concatenate([bias_ref[u, tile], bias_ref[u, tile]], axis=1)

    @functools.cache
    def stacked_queries(u, i):
        q = q_ref[i * t:(i + 1) * t, u * w:(u + 1) * w]
        zero = jnp.zeros_like(q)
        return jnp.concatenate([jnp.where(lane < DIFF_HEAD_DIM, q, zero),
                                jnp.where(lane >= DIFF_HEAD_DIM, q, zero)], axis=0)

    def score_chunk(u, i, c):
        s = lax.dot_general(k_ref[c * t:(c + 1) * t, u * w:(u + 1) * w], stacked_queries(u, i),
                            (((1,), (1,)), ((), ())), preferred_element_type=jnp.float32)
        return s + bias2(u, 1) if c == i else s + bias2(u, 0) if c == i - 1 else s

    order = list(range(0, n_blocks, 2)) + list(range(n_blocks - 1 - n_blocks % 2, 0, -2))
    work = [(u, i) for u in range(DIFF_STEP_HEADS) for i in order]
    chunks = [score_chunk(*work[0], c) for c in range(work[0][1] + 1)]
    for pos, (u, i) in enumerate(work):
        nxt = work[pos + 1] if pos + 1 < len(work) else None
        m = functools.reduce(jnp.maximum, [_column_max(s) for s in chunks])
        acc = None
        next_chunks = []
        for c in range(max(i + 1, 0 if nxt is None else nxt[1] + 1)):
            if nxt is not None and c <= nxt[1]:
                next_chunks.append(score_chunk(*nxt, c))
            if c <= i:
                p = jnp.exp2(chunks[c] - m).astype(v_ref.dtype)
                pv = jnp.dot(values_t(u)[:, c * t:(c + 1) * t], p, preferred_element_type=jnp.float32)
                acc = pv if acc is None else acc + pv
        chunks = next_chunks
        o2_t = acc[:w] / acc[w:w + 1]
        o = (o2_t[:, :t] - lam * o2_t[:, t:]).T
        ms = jnp.mean(o * o, axis=-1, keepdims=True)
        o = o * lax.rsqrt(ms + DIFF_SUBLN_EPS) * g_ref[...]
        o_ref[i * t:(i + 1) * t, u * w:(u + 1) * w] = (o * (1.0 - lam_init)).astype(o_ref.dtype)


def _diff_attention(z3, lam_vecs, subln_g, bias, layer):
    b, s, _ = z3.shape
    t = DIFF_TILE
    w = DIFF_STEP_HEADS * DIFF_V_DIM
    return pl.pallas_call(
        functools.partial(_diff_kernel, layer=layer),
        grid=(b, DIFF_HEADS // DIFF_STEP_HEADS),
        in_specs=[pl.BlockSpec((None, 4, DIFF_HEAD_DIM), lambda bi, h: (layer, 0, 0)),
                  pl.BlockSpec((None, 1, DIFF_V_DIM), lambda bi, h: (layer, 0, 0)),
                  pl.BlockSpec((None, s, w), lambda bi, h: (bi, 0, Z_QB // w + h)),
                  pl.BlockSpec((None, s, w), lambda bi, h: (bi, 0, Z_KB // w + h)),
                  pl.BlockSpec((None, s, w), lambda bi, h: (bi, 0, Z_VB // w + h)),
                  pl.BlockSpec((DIFF_STEP_HEADS, 2, t, t), lambda bi, h: (h, 0, 0, 0))],
        out_specs=pl.BlockSpec((None, s, w), lambda bi, h: (bi, 0, h)),
        out_shape=jax.ShapeDtypeStruct((b, s, DIFF_V), jnp.bfloat16),
        compiler_params=_params(("parallel", "parallel")),
        name="diff_attention",
    )(lam_vecs, subln_g, z3, z3, z3, bias)


def _mem_kernel(q_ref, kv_ref, o_ref):
    w = MEM_HEAD_DIM
    for h in range(MEM_HEADS):
        q = q_ref[0, :, h * w:(h + 1) * w]
        k = kv_ref[:, h * w:(h + 1) * w]
        v = kv_ref[:, MEM_Q + h * w:MEM_Q + (h + 1) * w]
        s = lax.dot_general(q, k, (((1,), (1,)), ((), ())), preferred_element_type=jnp.float32)
        m = jnp.max(s, axis=-1, keepdims=True)
        p = jnp.exp2(s - m)
        l = jnp.sum(p, axis=-1, keepdims=True)
        o = jnp.dot(p.astype(v.dtype), v, preferred_element_type=jnp.float32)
        o_ref[:, h * w:(h + 1) * w] = (o / l).astype(o_ref.dtype)


def _mem_attention(z3, mem_kv3, *, tq):
    b, s, _ = z3.shape
    return pl.pallas_call(
        _mem_kernel,
        grid=(b, s // tq),
        in_specs=[pl.BlockSpec((pl.Element(1), pl.Element(tq), pl.Element(MEM_Q)),
                               lambda bi, i: (bi, i * tq, Z_QC)),
                  pl.BlockSpec((None, N_MEM, 2 * MEM_Q), lambda bi, i: (bi, 0, 0))],
        out_specs=pl.BlockSpec((None, tq, MEM_Q), lambda bi, i: (bi, i, 0)),
        out_shape=jax.ShapeDtypeStruct((b, s, MEM_Q), jnp.bfloat16),
        compiler_params=_params(("parallel", "parallel")),
        name="mem_attention",
    )(z3, mem_kv3)


def _merge_kernel(x_ref, gate_ref, oa_ref, ob_ref, oc_ref, wb_ref, wo_ref, o_ref):
    merged = None
    for br, br_ref in enumerate((oa_ref, ob_ref, oc_ref)):
        proj = jnp.dot(br_ref[...], wb_ref[br], preferred_element_type=jnp.float32)
        gate = gate_ref[:, br * D_MODEL:(br + 1) * D_MODEL].astype(jnp.float32)
        term = gate * proj
        merged = term if merged is None else merged + term
    y = jnp.dot(merged.astype(wo_ref.dtype), wo_ref[...], preferred_element_type=jnp.float32)
    o_ref[...] = x_ref[...] + y


def _merge_out(x, z, o_a, o_b, o_c, w_branch, w_out, layer, *, tm):
    m, d = x.shape
    once = pl.Buffered(1)
    return pl.pallas_call(
        _merge_kernel,
        grid=(m // tm,),
        in_specs=[pl.BlockSpec((tm, d), lambda i: (i, 0)),
                  pl.BlockSpec((pl.Element(tm), pl.Element(GATE_W)), lambda i: (i * tm, Z_GATE)),
                  pl.BlockSpec((tm, BRANCH_WIDTH), lambda i: (i, 0)),
                  pl.BlockSpec((tm, BRANCH_WIDTH), lambda i: (i, 0)),
                  pl.BlockSpec((tm, BRANCH_WIDTH), lambda i: (i, 0)),
                  pl.BlockSpec((None, N_BRANCH, BRANCH_WIDTH, d), lambda i: (layer, 0, 0, 0),
                               pipeline_mode=once),
                  pl.BlockSpec((None, d, d), lambda i: (layer, 0, 0), pipeline_mode=once)],
        out_specs=pl.BlockSpec((tm, d), lambda i: (i, 0)),
        out_shape=jax.ShapeDtypeStruct((m, d), jnp.float32),
        compiler_params=_params(("parallel",)),
        name="merge_out",
    )(x, z, o_a, o_b, o_c, w_branch, w_out)


def _mlp_kernel(x_ref, g_ref, w1_ref, w2_ref, fg_ref, o_ref, h_ref, *, final_norm):
    j = pl.program_id(1)

    @pl.when(j == 0)
    def _():
        _store_rmsnorm(h_ref, x_ref, g_ref)
        o_ref[...] = x_ref[...]

    u = jnp.dot(h_ref[...], w1_ref[...], preferred_element_type=jnp.float32)
    u = jnp.square(jnp.maximum(u, 0.0)).astype(w2_ref.dtype)
    o_ref[...] += jnp.dot(u, w2_ref[...], preferred_element_type=jnp.float32)

    if final_norm:
        @pl.when(j == pl.num_programs(1) - 1)
        def _():
            y = o_ref[...]
            ms = jnp.mean(y * y, axis=-1, keepdims=True)
            o_ref[...] = y * lax.rsqrt(ms + RMS_EPS) * fg_ref[...]


def _mlp(x, g, w1, w2, final_g, layer, *, tm, tf, final_norm):
    m, d = x.shape
    f = w1.shape[-1]
    return pl.pallas_call(
        functools.partial(_mlp_kernel, final_norm=final_norm),
        grid=(m // tm, f // tf),
        in_specs=[pl.BlockSpec((tm, d), lambda i, j: (i, 0)),
                  pl.BlockSpec((None, 1, d), lambda i, j: (layer, 0, 0)),
                  pl.BlockSpec((None, d, tf), lambda i, j: (layer, 0, j)),
                  pl.BlockSpec((None, tf, d), lambda i, j: (layer, j, 0)),
                  pl.BlockSpec((1, d), lambda i, j: (0, 0))],
        out_specs=pl.BlockSpec((tm, d), lambda i, j: (i, 0)),
        out_shape=jax.ShapeDtypeStruct((m, d), jnp.float32),
        scratch_shapes=[pltpu.VMEM((tm, d), jnp.bfloat16)],
        compiler_params=_params(("parallel", "arbitrary")),
        name="mlp",
    )(x, g, w1, w2, final_g)


def kernel(x, mem, rel_bias, norm1_g, w_in, b_gate, swa_sinks, lam_q1, lam_k1, lam_q2, lam_k2,
           diff_subln_g, mem_norm_g, w_mem_kv, w_branch, w_out, norm2_g, w_mlp1, w_mlp2, final_g):
    bsz, seq, d = x.shape
    m = bsz * seq
    bf = jnp.bfloat16

    w_in_z = w_in.astype(bf)
    w_mem_kv_b = w_mem_kv.astype(bf)
    w_branch_b = w_branch.astype(bf)
    w_out_b = w_out.astype(bf)
    w_mlp1_b = w_mlp1.astype(bf)
    w_mlp2_b = w_mlp2.astype(bf)

    norm1_g3 = norm1_g.reshape(DEPTH, 1, d)
    norm2_g3 = norm2_g.reshape(DEPTH, 1, d)
    mem_norm_g3 = mem_norm_g.reshape(DEPTH, 1, d)
    b_gate3 = b_gate.reshape(DEPTH, 1, GATE_W)
    subln_g3 = diff_subln_g.reshape(DEPTH, 1, DIFF_V_DIM)
    lam_vecs = jnp.stack([lam_q1, lam_k1, lam_q2, lam_k2], axis=1)
    final_g2 = final_g.reshape(1, d)
    col_scale = jnp.asarray(_z_column_scale())

    swa_bias = _bias_tiles(rel_bias, _swa_bucket_maps(), SWA_HEADS, 0, relative=False)
    diff_bias = _bias_tiles(rel_bias, _diff_bucket_maps(), DIFF_HEADS, SWA_HEADS, relative=True)

    x2 = x.reshape(m, d)
    mem2 = mem.reshape(bsz * N_MEM, d)
    for layer in range(DEPTH):
        z = _proj_in(x2, norm1_g3, w_in_z, b_gate3, col_scale, layer, tm=1024, tn=768)
        z3 = z.reshape(bsz, seq, D_IN)
        mem_kv = _norm_matmul(mem2, mem_norm_g3, w_mem_kv_b, layer, tm=1024, tn=1024)
        o_a = _swa_attention(z3, swa_sinks, swa_bias, layer)
        o_b = _diff_attention(z3, lam_vecs, subln_g3, diff_bias, layer)
        o_c = _mem_attention(z3, mem_kv.reshape(bsz, N_MEM, 2 * MEM_Q), tq=1024)
        x2 = _merge_out(x2, z, o_a.reshape(m, SWA_Q), o_b.reshape(m, DIFF_V), o_c.reshape(m, MEM_Q),
                        w_branch_b, w_out_b, layer, tm=256)
        x2 = _mlp(x2, norm2_g3, w_mlp1_b, w_mlp2_b, final_g2, layer, tm=1024, tf=512,
                  final_norm=(layer == DEPTH - 1))
    return x2.reshape(bsz, seq, d)
```
